```python
import jax, jax.numpy as jnp
from jax import lax
import numpy as np

D_MODEL = 1024
BATCH = 8
SEQ = 8192
DEPTH = 4

HEAD_DIM = 64
N_FOX_HEADS = D_MODEL // (2 * HEAD_DIM)
N_RET_HEADS = D_MODEL // (2 * HEAD_DIM)
FOX_WIDTH = N_FOX_HEADS * HEAD_DIM
RET_WIDTH = N_RET_HEADS * HEAD_DIM
MIX_WIDTH = FOX_WIDTH + RET_WIDTH
D_FF = -(-8 * D_MODEL // (3 * 128)) * 128
CONV_WIDTH = 3
PLE_DIM = 256
Q_BLOCK = 128
RET_CHUNK = 128
ROPE_BASE = 10000.0
NORM_EPS = 1e-6
GN_EPS = 1e-5
IN_SIZES = (FOX_WIDTH, FOX_WIDTH, FOX_WIDTH, N_FOX_HEADS, RET_WIDTH, RET_WIDTH, RET_WIDTH, RET_WIDTH)
IN_WIDTH = sum(IN_SIZES)
SPLIT_POINTS = tuple(int(v) for v in np.cumsum(IN_SIZES)[:-1])

kernel_name = "hymba_fox_retnet_convffn_ple"


def rms_norm(x, w):
    xf = x.astype(jnp.float32)
    y = xf * lax.rsqrt(jnp.mean(xf * xf, axis=-1, keepdims=True) + NORM_EPS)
    return (y * w.astype(jnp.float32)).astype(x.dtype)


def split_heads(t, n_heads):
    b, s, _ = t.shape
    return t.reshape(b, s, n_heads, HEAD_DIM)


def rotary_tables(seq):
    inv_freq = ROPE_BASE ** (-jnp.arange(0, HEAD_DIM, 2, dtype=jnp.float32) / HEAD_DIM)
    ang = jnp.arange(seq, dtype=jnp.float32)[:, None] * inv_freq[None, :]
    return jnp.cos(ang), jnp.sin(ang)


def apply_rotary(t, cos, sin):
    half = HEAD_DIM // 2
    t1, t2 = t[..., :half], t[..., half:]
    c = cos[None, :, None, :]
    s = sin[None, :, None, :]
    return jnp.concatenate([t1 * c - t2 * s, t1 * s + t2 * c], axis=-1).astype(t.dtype)


def forgetting_attention(q, k, v, f_logit):
    b, s, h, d = q.shape
    q = q.transpose(0, 2, 1, 3)
    k = k.transpose(0, 2, 1, 3)
    v = v.transpose(0, 2, 1, 3)
    log_f = jax.nn.log_sigmoid(f_logit.astype(jnp.float32))
    cum = jnp.cumsum(log_f, axis=1).transpose(0, 2, 1)
    scale = HEAD_DIM ** -0.5
    kpos = jnp.arange(s)
    n_blocks = s // Q_BLOCK

    def one_block(bi):
        start = bi * Q_BLOCK
        qb = lax.dynamic_slice_in_dim(q, start, Q_BLOCK, axis=2)
        cq = lax.dynamic_slice_in_dim(cum, start, Q_BLOCK, axis=2)
        logits = jnp.einsum('bhqd,bhkd->bhqk', qb, k).astype(jnp.float32) * scale
        logits = logits + cq[..., :, None] - cum[..., None, :]
        qpos = start + jnp.arange(Q_BLOCK)
        logits = jnp.where(kpos[None, :] <= qpos[:, None], logits, -jnp.inf)
        probs = jax.nn.softmax(logits, axis=-1)
        return jnp.einsum('bhqk,bhkd->bhqd', probs.astype(v.dtype), v)

    out = lax.map(one_block, jnp.arange(n_blocks))
    return out.transpose(1, 0, 3, 2, 4).reshape(b, s, h * d)


def chunkwise_retention(q, k, v, log_gamma):
    b, s, h, d = q.shape
    n_chunks = s // RET_CHUNK

    def to_chunks(t):
        return t.reshape(b, n_chunks, RET_CHUNK, h, d).transpose(1, 0, 3, 2, 4)

    idx = jnp.arange(RET_CHUNK, dtype=jnp.float32)
    rel = idx[:, None] - idx[None, :]
    lg = log_gamma[:, None, None]
    intra_decay = jnp.where(rel >= 0, jnp.exp(lg * jnp.maximum(rel, 0.0)), 0.0)
    q_decay = jnp.exp(log_gamma[:, None] * (idx + 1.0))
    k_decay = jnp.exp(log_gamma[:, None] * (RET_CHUNK - 1.0 - idx))
    chunk_decay = jnp.exp(log_gamma * RET_CHUNK)

    def step(state, qkv):
        qc, kc, vc = qkv
        scores = jnp.einsum('bhid,bhjd->bhij', qc, kc) * intra_decay
        inner = jnp.einsum('bhij,bhje->bhie', scores, vc)
        cross = jnp.einsum('bhid,bhde->bhie', qc, state) * q_decay[:, :, None]
        new_state = state * chunk_decay[:, None, None] + jnp.einsum(
            'bhjd,bhje->bhde', kc * k_decay[:, :, None], vc)
        return new_state, inner + cross

    state0 = jnp.zeros((b, h, d, d), jnp.float32)
    _, out = lax.scan(step, state0, (to_chunks(q), to_chunks(k), to_chunks(v)))
    return out.transpose(1, 0, 3, 2, 4).reshape(b, s, h, d)


def head_group_norm(y):
    yf = y.astype(jnp.float32)
    mu = jnp.mean(yf, axis=-1, keepdims=True)
    var = jnp.mean(jnp.square(yf - mu), axis=-1, keepdims=True)
    return (yf - mu) * lax.rsqrt(var + GN_EPS)


def causal_depthwise_conv(a, w, bias):
    s = a.shape[1]
    ap = jnp.pad(a, ((0, 0), (CONV_WIDTH - 1, 0), (0, 0)))
    y = bias
    for j in range(CONV_WIDTH):
        y = y + ap[:, j:j + s, :] * w[j]
    return y


def gated_conv_ffn(u, w_up, conv_w, conv_b, w_down):
    up = u @ w_up
    a, g = up[..., :D_FF], up[..., D_FF:]
    a = causal_depthwise_conv(a, conv_w, conv_b)
    return (jax.nn.gelu(a, approximate=False) * g) @ w_down


def setup_inputs(seed: int = 0) -> dict:
    key = jax.random.key(seed)
    ks = jax.random.split(key, 16)
    nrm = jax.random.normal
    f32 = jnp.float32
    return {
        'x': nrm(ks[0], (BATCH, SEQ, D_MODEL), f32),
        'p': nrm(ks[1], (DEPTH, BATCH, SEQ, PLE_DIM), f32),
        'attn_norm_w': 1.0 + 0.05 * nrm(ks[2], (DEPTH, D_MODEL), f32),
        'w_in': nrm(ks[3], (DEPTH, D_MODEL, IN_WIDTH), f32) * D_MODEL ** -0.5,
        'forget_bias': 2.0 + 0.1 * nrm(ks[4], (DEPTH, N_FOX_HEADS), f32),
        'w_out': nrm(ks[5], (DEPTH, MIX_WIDTH, D_MODEL), f32) * MIX_WIDTH ** -0.5,
        'ffn_norm_w': 1.0 + 0.05 * nrm(ks[6], (DEPTH, D_MODEL), f32),
        'w_up': nrm(ks[7], (DEPTH, D_MODEL, 2 * D_FF), f32) * D_MODEL ** -0.5,
        'conv_w': nrm(ks[8], (DEPTH, CONV_WIDTH, D_FF), f32) * CONV_WIDTH ** -0.5,
        'conv_b': 0.01 * nrm(ks[9], (DEPTH, D_FF), f32),
        'w_down': nrm(ks[10], (DEPTH, D_FF, D_MODEL), f32) * D_FF ** -0.5,
        'ple_norm_w': 1.0 + 0.05 * nrm(ks[11], (DEPTH, D_MODEL), f32),
        'w_ple_gate': nrm(ks[12], (DEPTH, D_MODEL, D_MODEL), f32) * D_MODEL ** -0.5,
        'w_ple_proj': nrm(ks[13], (DEPTH, PLE_DIM, D_MODEL), f32) * PLE_DIM ** -0.5,
        'final_norm_w': 1.0 + 0.05 * nrm(ks[14], (D_MODEL,), f32),
    }


def reference(x, p, attn_norm_w, w_in, forget_bias, w_out, ffn_norm_w, w_up, conv_w, conv_b,
              w_down, ple_norm_w, w_ple_gate, w_ple_proj, final_norm_w):
    seq = x.shape[1]
    cos, sin = rotary_tables(seq)
    log_gamma = jnp.log1p(-jnp.exp2(-5.0 - jnp.arange(N_RET_HEADS, dtype=jnp.float32)))
    h = x
    for i in range(DEPTH):
        u = rms_norm(h, attn_norm_w[i])
        proj = u @ w_in[i]
        fq, fk, fv, f_logit, rq, rk, rv, rg = jnp.split(proj, SPLIT_POINTS, axis=-1)
        fox = forgetting_attention(split_heads(fq, N_FOX_HEADS), split_heads(fk, N_FOX_HEADS),
                                   split_heads(fv, N_FOX_HEADS), f_logit + forget_bias[i])
        rq_h = apply_rotary(split_heads(rq, N_RET_HEADS), cos, sin)
        rk_h = apply_rotary(split_heads(rk, N_RET_HEADS), cos, sin) * (HEAD_DIM ** -0.5)
        ret = chunkwise_retention(rq_h, rk_h, split_heads(rv, N_RET_HEADS), log_gamma)
        ret = head_group_norm(ret).reshape(ret.shape[0], seq, RET_WIDTH)
        ret = (jax.nn.silu(rg.astype(jnp.float32)) * ret).astype(x.dtype)
        mixed = jnp.concatenate([fox.astype(x.dtype), ret], axis=-1)
        h = h + mixed @ w_out[i]
        h = h + gated_conv_ffn(rms_norm(h, ffn_norm_w[i]), w_up[i], conv_w[i], conv_b[i], w_down[i])
        gate = jax.nn.sigmoid(rms_norm(h, ple_norm_w[i]) @ w_ple_gate[i])
        h = h + gate * (p[i] @ w_ple_proj[i])
    return rms_norm(h, final_norm_w)
```

```python
import functools

import numpy as np
import jax
import jax.numpy as jnp
from jax import lax
from jax.experimental import pallas as pl
from jax.experimental.pallas import tpu as pltpu

F32 = jnp.float32
BF16 = jnp.bfloat16

HEAD_DIM = 64
HALF = HEAD_DIM // 2
N_HEADS = 8
WIDTH = N_HEADS * HEAD_DIM
LANES = 128
N_PAIRS = WIDTH // LANES
BF16_ROWS = 16
CONV_WIDTH = 3
ROPE_BASE = 10000.0
NORM_EPS = 1e-6
GN_EPS = 1e-5
SEQ_TILE = 512
RET_CHUNK = 256
FF_CHUNK = 1024
VMEM_LIMIT = 60 * 1024 * 1024


def _rms(x, w):
    return x * lax.rsqrt(jnp.mean(x * x, axis=-1, keepdims=True) + NORM_EPS) * w


def _split3(x):
    hi = x.astype(BF16)
    r = x - hi.astype(F32)
    mid = r.astype(BF16)
    lo = (r - mid.astype(F32)).astype(BF16)
    return hi, mid, lo


def _inproj_kernel(h_ref, nw_ref, wn_ref, wt_ref, fb_ref, cosn_ref, sina_ref, sinb_ref, cost_ref, sint_ref,
                   fq_ref, fv_ref, rq_ref, rv_ref, rg_ref, fkt_ref, rkt_ref, cum_ref, carry_ref):
    tm = h_ref.shape[0]

    @pl.when(pl.program_id(1) == 0)
    def _():
        carry_ref[...] = jnp.zeros_like(carry_ref)

    u = _rms(h_ref[...], nw_ref[...]).astype(BF16)
    pn = jnp.dot(u, wn_ref[...], preferred_element_type=F32)
    pt = lax.dot_general(wt_ref[...], u, (((1,), (1,)), ((), ())),
                         preferred_element_type=F32)

    fq_ref[...] = pn[:, 0:WIDTH].astype(BF16)
    fv_ref[...] = pn[:, WIDTH:2 * WIDTH].astype(BF16)
    rv_ref[...] = pn[:, 3 * WIDTH:4 * WIDTH].astype(BF16)
    rg = pn[:, 4 * WIDTH:5 * WIDTH]
    rg_ref[...] = (rg * jax.nn.sigmoid(rg)).astype(BF16)

    cosn, sina, sinb = cosn_ref[...], sina_ref[...], sinb_ref[...]
    for g in range(N_PAIRS):
        xg = pn[:, 2 * WIDTH + g * LANES:2 * WIDTH + (g + 1) * LANES]
        o = xg * cosn + pltpu.roll(xg, LANES - HALF, 1) * sina + pltpu.roll(xg, HALF, 1) * sinb
        rq_ref[:, g * LANES:(g + 1) * LANES] = o.astype(BF16)

    fkt_ref[...] = pt[0:WIDTH].astype(BF16)
    cost, sint = cost_ref[...], sint_ref[...]
    for hd in range(N_HEADS):
        r0 = WIDTH + hd * HEAD_DIM
        t1, t2 = pt[r0:r0 + HALF], pt[r0 + HALF:r0 + HEAD_DIM]
        rkt_ref[hd * HEAD_DIM:hd * HEAD_DIM + HALF, :] = (t1 * cost - t2 * sint).astype(BF16)
        rkt_ref[hd * HEAD_DIM + HALF:(hd + 1) * HEAD_DIM, :] = (t1 * sint + t2 * cost).astype(BF16)

    logit = pt[2 * WIDTH:2 * WIDTH + N_HEADS] + fb_ref[...]
    logf = jnp.minimum(logit, 0.0) - jnp.log1p(jnp.exp(-jnp.abs(logit)))
    hi, mid, lo = _split3(logf)
    parts = jnp.concatenate([hi.astype(F32), mid.astype(F32), lo.astype(F32),
                             jnp.zeros_like(logf)], axis=0).astype(BF16)
    row = lax.broadcasted_iota(jnp.int32, (tm, tm), 0)
    col = lax.broadcasted_iota(jnp.int32, (tm, tm), 1)
    tri = jnp.where(row <= col, 1.0, 0.0).astype(BF16)
    cs = jnp.dot(parts, tri, preferred_element_type=F32)
    cum = cs[0:8] + cs[8:16] + cs[16:24] + carry_ref[...]
    cum_ref[...] = cum
    carry_ref[...] = carry_ref[...] + jnp.sum(logf, axis=1, keepdims=True)


def _inproj(h, nw, wn, wt, fb, tabs, tm):
    B, S, D = h.shape
    ns = S // tm
    cosn, sina, sinb, cost, sint = tabs
    tok = lambda w: pl.BlockSpec((None, tm, w), lambda b, s: (b, s, 0))
    full = lambda a: pl.BlockSpec(a.shape, lambda b, s: (0,) * a.ndim)
    ktile = pl.BlockSpec((None, None, WIDTH, tm), lambda b, s: (b, s, 0, 0))
    nat = jax.ShapeDtypeStruct((B, S, WIDTH), BF16)
    ktt = jax.ShapeDtypeStruct((B, ns, WIDTH, tm), BF16)
    return pl.pallas_call(
        _inproj_kernel,
        grid=(B, ns),
        in_specs=[tok(D), full(nw), full(wn), full(wt), full(fb),
                  pl.BlockSpec((tm, LANES), lambda b, s: (s, 0)),
                  pl.BlockSpec((tm, LANES), lambda b, s: (s, 0)),
                  pl.BlockSpec((tm, LANES), lambda b, s: (s, 0)),
                  pl.BlockSpec((HALF, tm), lambda b, s: (0, s)),
                  pl.BlockSpec((HALF, tm), lambda b, s: (0, s))],
        out_specs=[tok(WIDTH), tok(WIDTH), tok(WIDTH), tok(WIDTH), tok(WIDTH), ktile, ktile,
                   pl.BlockSpec((None, N_HEADS, tm), lambda b, s: (b, 0, s))],
        out_shape=[nat, nat, nat, nat, nat, ktt, ktt, jax.ShapeDtypeStruct((B, N_HEADS, S), F32)],
        scratch_shapes=[pltpu.VMEM((N_HEADS, 1), F32)],
        compiler_params=pltpu.CompilerParams(dimension_semantics=("parallel", "arbitrary"),
                                             vmem_limit_bytes=VMEM_LIMIT),
        name="inproj",
    )(h, nw, wn, wt, fb, cosn, sina, sinb, cost, sint)


def _fox_kernel(q_ref, kt_ref, v_ref, cum_ref, o_ref, kaug_ref, m_ref, l_ref, acc_ref):
    nk, _, tk = kt_ref.shape
    tq = tk
    aug_row = lax.broadcasted_iota(jnp.int32, (BF16_ROWS, tk), 0)

    for j in range(nk):
        kt = kt_ref[j]
        negc = -cum_ref[:, j * tk:(j + 1) * tk]
        terms = [t.astype(F32) for t in _split3(negc)]
        for hd in range(2):
            aug = jnp.where(aug_row == 0, terms[0][hd:hd + 1],
                            jnp.where(aug_row == 1, terms[1][hd:hd + 1],
                                      jnp.where(aug_row == 2, terms[2][hd:hd + 1], 0.0))).astype(BF16)
            zeros = jnp.zeros((HEAD_DIM - BF16_ROWS, tk), BF16)
            if hd == 0:
                kaug_ref[0, j, 0:HEAD_DIM, :] = kt[0:HEAD_DIM]
                kaug_ref[0, j, HEAD_DIM:HEAD_DIM + BF16_ROWS, :] = aug
                kaug_ref[0, j, HEAD_DIM + BF16_ROWS:LANES, :] = zeros
            else:
                kaug_ref[1, j, 0:BF16_ROWS, :] = aug
                kaug_ref[1, j, BF16_ROWS:HEAD_DIM, :] = zeros
                kaug_ref[1, j, HEAD_DIM:LANES, :] = kt[HEAD_DIM:LANES]

    lane = lax.broadcasted_iota(jnp.int32, (tq, LANES), 1)
    causal = lax.broadcasted_iota(jnp.int32, (tq, tk), 1) <= lax.broadcasted_iota(jnp.int32, (tq, tk), 0)
    n_ct = tk // LANES

    def q_block(qi, carry):
        q0 = pl.multiple_of(qi * tq, tq)
        q2 = q_ref[pl.ds(q0, tq), :].astype(F32)
        outs = []
        for hd in range(2):
            if hd == 0:
                qm = jnp.where(lane < HEAD_DIM, q2, jnp.where(lane < HEAD_DIM + 3, 1.0, 0.0)).astype(BF16)
            else:
                qm = jnp.where(lane >= HEAD_DIM, q2, jnp.where(lane < 3, 1.0, 0.0)).astype(BF16)
            m_ref[...] = jnp.full(m_ref.shape, -jnp.inf, F32)
            l_ref[...] = jnp.zeros(l_ref.shape, F32)
            acc_ref[...] = jnp.zeros(acc_ref.shape, F32)

            def k_step(kj, masked):
                s = jnp.dot(qm, kaug_ref[hd, kj], preferred_element_type=F32)
                if masked:
                    s = jnp.where(causal, s, -jnp.inf)
                cols = [s[:, c * LANES:(c + 1) * LANES] for c in range(n_ct)]
                mx = functools.reduce(jnp.maximum, cols)
                m_prev = m_ref[...]
                m_new = jnp.maximum(m_prev, jnp.max(mx, axis=1, keepdims=True))
                alpha = jnp.exp(m_prev - m_new)
                ps = [jnp.exp(c - m_new) for c in cols]
                pb = jnp.concatenate([p.astype(BF16) for p in ps], axis=1)
                k0 = pl.multiple_of(kj * tk, tk)
                pv = jnp.dot(pb, v_ref[pl.ds(k0, tk), :], preferred_element_type=F32)
                l_ref[...] = alpha * l_ref[...] + functools.reduce(jnp.add, ps)
                acc_ref[...] = alpha * acc_ref[...] + pv
                m_ref[...] = m_new

            def k_body(kj, c):
                k_step(kj, False)
                return c

            lax.fori_loop(0, qi, k_body, 0)
            k_step(qi, True)
            outs.append(acc_ref[...] / jnp.sum(l_ref[...], axis=1, keepdims=True))
        o_ref[pl.ds(q0, tq), :] = jnp.where(lane < HEAD_DIM, outs[0], outs[1]).astype(o_ref.dtype)
        return carry

    lax.fori_loop(0, nk, q_block, 0)


def _fox(fq, fkt, fv, cum):
    B, S, _ = fq.shape
    _, nk, _, tk = fkt.shape
    cum4 = cum.reshape(B, N_PAIRS, 2, S)
    nat = pl.BlockSpec((None, S, LANES), lambda b, p: (b, 0, p))
    return pl.pallas_call(
        _fox_kernel,
        grid=(B, N_PAIRS),
        in_specs=[nat,
                  pl.BlockSpec((None, nk, LANES, tk), lambda b, p: (b, 0, p, 0)),
                  nat,
                  pl.BlockSpec((None, None, 2, S), lambda b, p: (b, p, 0, 0))],
        out_specs=nat,
        out_shape=jax.ShapeDtypeStruct((B, S, WIDTH), BF16),
        scratch_shapes=[pltpu.VMEM((2, nk, LANES, tk), BF16),
                        pltpu.VMEM((tk, LANES), F32), pltpu.VMEM((tk, LANES), F32), pltpu.VMEM((tk, LANES), F32)],
        compiler_params=pltpu.CompilerParams(dimension_semantics=("parallel", "parallel"),
                                             vmem_limit_bytes=VMEM_LIMIT),
        name="fox_attention",
    )(fq, fkt, fv, cum4)


def _ret_kernel(q_ref, kt_ref, v_ref, g_ref, dm_ref, qd_ref, kd_ref, cd_ref, bd_ref, gn_ref, o_ref, st_ref):
    nk, _, tk = kt_ref.shape
    C = dm_ref.shape[-1]
    lane = lax.broadcasted_iota(jnp.int32, (C, LANES), 1)
    st_ref[...] = jnp.zeros_like(st_ref)
    gn = gn_ref[...]

    def group_mean(y):
        y_hi = y.astype(BF16)
        y_lo = (y - y_hi.astype(F32)).astype(BF16)
        return (jnp.dot(y_hi, gn, preferred_element_type=F32) + jnp.dot(y_lo, gn, preferred_element_type=F32))

    def tile_body(j, carry):
        kt_tile = kt_ref[j]
        for sc in range(tk // C):
            r0 = pl.multiple_of(j * tk + sc * C, C)
            q2 = q_ref[pl.ds(r0, C), :]
            v2 = v_ref[pl.ds(r0, C), :]
            kt = kt_tile[:, sc * C:(sc + 1) * C]
            q2f = q2.astype(F32)
            inner = []
            for hd in range(2):
                keep = (lane < HEAD_DIM) if hd == 0 else (lane >= HEAD_DIM)
                qm = jnp.where(keep, q2f, 0.0).astype(BF16)
                sc_h = jnp.dot(qm, kt, preferred_element_type=F32) * dm_ref[hd]
                inner.append(jnp.dot(sc_h.astype(BF16), v2, preferred_element_type=F32))
            st = st_ref[...]
            cross = jnp.dot(q2, st.astype(BF16), preferred_element_type=F32) * qd_ref[...]
            y = jnp.where(lane < HEAD_DIM, inner[0], inner[1]) + cross
            kdec = (kt.astype(F32) * kd_ref[...]).astype(BF16)
            st_ref[...] = st * cd_ref[...] + jnp.dot(kdec, v2, preferred_element_type=F32) * bd_ref[...]
            d = y - group_mean(y)
            var = group_mean(d * d)
            out = d * lax.rsqrt(var + GN_EPS) * g_ref[pl.ds(r0, C), :].astype(F32)
            o_ref[pl.ds(r0, C), :] = out.astype(o_ref.dtype)
        return carry

    lax.fori_loop(0, nk, tile_body, 0)


def _ret(rq, rkt, rv, rg, tabs):
    B, S, _ = rq.shape
    _, nk, _, tk = rkt.shape
    dm, qd, kd, cd, bd, gn = tabs
    C = dm.shape[-1]
    nat = pl.BlockSpec((None, S, LANES), lambda b, p: (b, 0, p))
    return pl.pallas_call(
        _ret_kernel,
        grid=(B, N_PAIRS),
        in_specs=[nat,
                  pl.BlockSpec((None, nk, LANES, tk), lambda b, p: (b, 0, p, 0)),
                  nat, nat,
                  pl.BlockSpec((None, 2, C, C), lambda b, p: (p, 0, 0, 0)),
                  pl.BlockSpec((None, C, LANES), lambda b, p: (p, 0, 0)),
                  pl.BlockSpec((None, LANES, C), lambda b, p: (p, 0, 0)),
                  pl.BlockSpec((None, LANES, LANES), lambda b, p: (p, 0, 0)),
                  pl.BlockSpec((LANES, LANES), lambda b, p: (0, 0)),
                  pl.BlockSpec((LANES, LANES), lambda b, p: (0, 0))],
        out_specs=nat,
        out_shape=jax.ShapeDtypeStruct((B, S, WIDTH), BF16),
        scratch_shapes=[pltpu.VMEM((LANES, LANES), F32)],
        compiler_params=pltpu.CompilerParams(dimension_semantics=("parallel", "parallel"),
                                             vmem_limit_bytes=VMEM_LIMIT),
        name="retention",
    )(rq, rkt, rv, rg, dm, qd, kd, cd, bd, gn)


def _mixer_kernel(h_ref, fox_ref, ret_ref, p_ref, wo_ref, fnw_ref, wa_ref, wg_ref, cw_ref, cb_ref, wd_ref,
                  pnw_ref, wpg_ref, wpp_ref, onw_ref, o_ref, abuf_ref, tail_ref, *, final_norm):
    tm = h_ref.shape[0]
    d_ff = wa_ref.shape[1]
    halo = tail_ref.shape[0]

    @pl.when(pl.program_id(1) == 0)
    def _():
        tail_ref[...] = jnp.zeros_like(tail_ref)

    h1 = (h_ref[...] + jnp.dot(fox_ref[...], wo_ref[0:WIDTH], preferred_element_type=F32)
          + jnp.dot(ret_ref[...], wo_ref[WIDTH:2 * WIDTH], preferred_element_type=F32))
    u = _rms(h1, fnw_ref[...]).astype(BF16)
    down = None
    for c0 in range(0, d_ff, FF_CHUNK):
        c1 = min(c0 + FF_CHUNK, d_ff)
        w = c1 - c0
        a = jnp.dot(u, wa_ref[:, c0:c1], preferred_element_type=F32)
        g = jnp.dot(u, wg_ref[:, c0:c1], preferred_element_type=F32)
        abuf_ref[0:halo, 0:w] = tail_ref[:, c0:c1]
        abuf_ref[halo:halo + tm, 0:w] = a
        tail_ref[:, c0:c1] = a[tm - halo:tm]
        y = cb_ref[:, c0:c1]
        for t in range(CONV_WIDTH):
            off = halo - (CONV_WIDTH - 1) + t
            y = y + abuf_ref[off:off + tm, 0:w] * cw_ref[t:t + 1, c0:c1]
        act = (0.5 * y * (1.0 + lax.erf(y * np.float32(1.0 / np.sqrt(2.0)))) * g).astype(BF16)
        part = jnp.dot(act, wd_ref[c0:c1, :], preferred_element_type=F32)
        down = part if down is None else down + part
    h2 = h1 + down
    n = _rms(h2, pnw_ref[...]).astype(BF16)
    gate = jax.nn.sigmoid(jnp.dot(n, wpg_ref[...], preferred_element_type=F32))
    pe = jnp.dot(p_ref[...].astype(BF16), wpp_ref[...], preferred_element_type=F32)
    h3 = h2 + gate * pe
    if final_norm:
        h3 = _rms(h3, onw_ref[...])
    o_ref[...] = h3


def _mixer(h, fox, ret, p, wo, fnw, wa, wg, cw, cb, wd, pnw, wpg, wpp, onw, tm, final_norm):
    B, S, D = h.shape
    tok = lambda w: pl.BlockSpec((None, tm, w), lambda b, s: (b, s, 0))
    res = lambda a: pl.BlockSpec(a.shape, lambda b, s: (0,) * a.ndim, pipeline_mode=pl.Buffered(1))
    return pl.pallas_call(
        functools.partial(_mixer_kernel, final_norm=final_norm),
        grid=(B, S // tm),
        in_specs=[tok(D), tok(WIDTH), tok(WIDTH), tok(p.shape[-1])]
                 + [res(a) for a in (wo, fnw, wa, wg, cw, cb, wd, pnw, wpg, wpp, onw)],
        out_specs=tok(D),
        out_shape=jax.ShapeDtypeStruct((B, S, D), F32),
        scratch_shapes=[pltpu.VMEM((tm + 8, FF_CHUNK), F32), pltpu.VMEM((8, wa.shape[1]), F32)],
        compiler_params=pltpu.CompilerParams(dimension_semantics=("parallel", "arbitrary"),
                                             vmem_limit_bytes=VMEM_LIMIT),
        name="channel_mixer",
    )(h, fox, ret, p, wo, fnw, wa, wg, cw, cb, wd, pnw, wpg, wpp, onw)


def _rotary_tables(seq):
    inv_freq = ROPE_BASE ** (-jnp.arange(0, HEAD_DIM, 2, dtype=F32) / HEAD_DIM)
    ang = jnp.arange(seq, dtype=F32)[:, None] * inv_freq[None, :]
    cos, sin = jnp.cos(ang), jnp.sin(ang)
    first = (jnp.arange(LANES) % HEAD_DIM) < HALF
    cosn = jnp.tile(cos, (1, LANES // HALF))
    sinn = jnp.tile(sin, (1, LANES // HALF))
    sina = jnp.where(first[None, :], -sinn, 0.0)
    sinb = jnp.where(first[None, :], 0.0, sinn)
    return cosn, sina, sinb, cos.T, sin.T


def _retention_tables(C):
    log_gamma = jnp.log1p(-jnp.exp2(-5.0 - jnp.arange(N_HEADS, dtype=F32)))
    idx = jnp.arange(C, dtype=F32)
    rel = idx[:, None] - idx[None, :]
    lg = log_gamma[:, None, None]
    dm = jnp.where(rel >= 0, jnp.exp(lg * jnp.maximum(rel, 0.0)), 0.0).reshape(N_PAIRS, 2, C, C)
    q_decay = jnp.exp(log_gamma[:, None] * (idx + 1.0))
    k_decay = jnp.exp(log_gamma[:, None] * (C - 1.0 - idx))
    chunk_decay = jnp.exp(log_gamma * C)
    qd = jnp.repeat(q_decay.reshape(N_PAIRS, 2, C), HEAD_DIM, axis=1).transpose(0, 2, 1)
    kd = jnp.repeat(k_decay.reshape(N_PAIRS, 2, C), HEAD_DIM, axis=1)
    blk = jnp.arange(LANES) // HEAD_DIM
    bd = (blk[:, None] == blk[None, :]).astype(F32)
    cd = jnp.repeat(chunk_decay.reshape(N_PAIRS, 2), HEAD_DIM, axis=1)[:, :, None] * bd[None]
    gn = (bd / HEAD_DIM).astype(BF16)
    return dm, qd, kd, cd, bd, gn


def kernel(x, p, attn_norm_w, w_in, forget_bias, w_out, ffn_norm_w, w_up, conv_w, conv_b, w_down, ple_norm_w,
           w_ple_gate, w_ple_proj, final_norm_w):
    B, S, D = x.shape
    depth = w_in.shape[0]
    d_ff = w_down.shape[1]
    tm = min(SEQ_TILE, S)
    assert S % tm == 0 and tm % RET_CHUNK == 0 and D == 2 * WIDTH
    scale = HEAD_DIM ** -0.5
    rot = _rotary_tables(S)
    rtab = _retention_tables(RET_CHUNK)

    o = [0, WIDTH, 2 * WIDTH, 3 * WIDTH, 3 * WIDTH + N_HEADS]
    o += [o[-1] + WIDTH, o[-1] + 2 * WIDTH, o[-1] + 3 * WIDTH, o[-1] + 4 * WIDTH]
    row = lambda v: v.reshape(1, -1)
    h = x
    for i in range(depth):
        wi = w_in[i]
        col = lambda k: wi[:, o[k]:o[k + 1]]
        wn = jnp.concatenate([col(0) * scale, col(2), col(4), col(6), col(7)], axis=1).astype(BF16)
        wt = jnp.concatenate([col(1), col(5) * scale, col(3), jnp.zeros((D, BF16_ROWS - N_HEADS), F32)],
                             axis=1).T.astype(BF16)
        fq, fv, rq, rv, rg, fkt, rkt, cum = _inproj(h, row(attn_norm_w[i]), wn, wt,
                                                    forget_bias[i].reshape(N_HEADS, 1), rot, tm)
        fox = _fox(fq, fkt, fv, cum)
        ret = _ret(rq, rkt, rv, rg, rtab)
        h = _mixer(h, fox, ret, p[i], w_out[i].astype(BF16), row(ffn_norm_w[i]),
                   w_up[i][:, :d_ff].astype(BF16), w_up[i][:, d_ff:].astype(BF16), conv_w[i], row(conv_b[i]),
                   w_down[i].astype(BF16), row(ple_norm_w[i]), w_ple_gate[i].astype(BF16),
                   w_ple_proj[i].astype(BF16), row(final_norm_w), tm, final_norm=(i == depth - 1))
    return h
```

```python
import functools

import numpy as np
import jax
import jax.numpy as jnp
from jax import lax
from jax.experimental import pallas as pl
from jax.experimental.pallas import tpu as pltpu

F32 = jnp.float32
BF16 = jnp.bfloat16

HEAD_DIM = 64
HALF = HEAD_DIM // 2
N_HEADS = 8
WIDTH = N_HEADS * HEAD_DIM
LANES = 128
N_PAIRS = WIDTH // LANES
BF16_ROWS = 16
N_AUG = 3
CONV_WIDTH = 3
ROPE_BASE = 10000.0
NORM_EPS = 1e-6
GN_EPS = 1e-5
LOG2E = float(np.log2(np.e))
SEQ_TILE = 512
FOX_SUB = 256
FOX_QTILES = 2
RET_CHUNK = 256
FF_CHUNK = 1024
VMEM_LIMIT = 60 * 1024 * 1024


def _rms(x, w):
    return x * lax.rsqrt(jnp.mean(x * x, axis=-1, keepdims=True) + NORM_EPS) * w


def _split3(x):
    hi = x.astype(BF16).astype(F32)
    r = x - hi
    mid = r.astype(BF16).astype(F32)
    lo = (r - mid).astype(BF16).astype(F32)
    return hi, mid, lo


def _inproj_kernel(h_ref, nw_ref, wn_ref, wt_ref, fb_ref, cosn_ref, sina_ref, sinb_ref, cost_ref, sint_ref,
                   fk_ref, rq_ref, rv_ref, rg_ref, fqt_ref, fvt_ref, rkt_ref, cum_ref, carry_ref):
    tm = h_ref.shape[0]

    @pl.when(pl.program_id(1) == 0)
    def _():
        carry_ref[...] = jnp.zeros_like(carry_ref)

    u = _rms(h_ref[...], nw_ref[...]).astype(BF16)
    pn = jnp.dot(u, wn_ref[...], preferred_element_type=F32)
    pt = lax.dot_general(wt_ref[...], u, (((1,), (1,)), ((), ())),
                         preferred_element_type=F32)

    fk_ref[...] = pn[:, 0:WIDTH].astype(BF16)
    rv_ref[...] = pn[:, 2 * WIDTH:3 * WIDTH].astype(BF16)
    rg = pn[:, 3 * WIDTH:4 * WIDTH]
    rg_ref[...] = (rg * jax.nn.sigmoid(rg)).astype(BF16)

    cosn, sina, sinb = cosn_ref[...], sina_ref[...], sinb_ref[...]
    for g in range(N_PAIRS):
        xg = pn[:, WIDTH + g * LANES:WIDTH + (g + 1) * LANES]
        o = xg * cosn + pltpu.roll(xg, LANES - HALF, 1) * sina + pltpu.roll(xg, HALF, 1) * sinb
        rq_ref[:, g * LANES:(g + 1) * LANES] = o.astype(BF16)

    fqt_ref[...] = pt[0:WIDTH].astype(BF16)
    fvt_ref[...] = pt[WIDTH:2 * WIDTH].astype(BF16)
    cost, sint = cost_ref[...], sint_ref[...]
    for hd in range(N_HEADS):
        r0 = 2 * WIDTH + hd * HEAD_DIM
        t1, t2 = pt[r0:r0 + HALF], pt[r0 + HALF:r0 + HEAD_DIM]
        rkt_ref[hd * HEAD_DIM:hd * HEAD_DIM + HALF, :] = (t1 * cost - t2 * sint).astype(BF16)
        rkt_ref[hd * HEAD_DIM + HALF:(hd + 1) * HEAD_DIM, :] = (t1 * sint + t2 * cost).astype(BF16)

    logit = pt[3 * WIDTH:3 * WIDTH + N_HEADS] + fb_ref[...]
    logf = jnp.minimum(logit, 0.0) - jnp.log1p(jnp.exp(-jnp.abs(logit)))
    parts = jnp.concatenate(list(_split3(logf)) + [jnp.zeros_like(logf)], axis=0).astype(BF16)
    row = lax.broadcasted_iota(jnp.int32, (tm, tm), 0)
    col = lax.broadcasted_iota(jnp.int32, (tm, tm), 1)
    tri = jnp.where(row <= col, 1.0, 0.0).astype(BF16)
    cs = jnp.dot(parts, tri, preferred_element_type=F32)
    cum_ref[...] = cs[0:8] + cs[8:16] + cs[16:24] + carry_ref[...]
    carry_ref[...] = carry_ref[...] + jnp.sum(logf, axis=1, keepdims=True)


def _inproj(h, nw, wn, wt, fb, tabs, tm):
    B, S, D = h.shape
    ns = S // tm
    cosn, sina, sinb, cost, sint = tabs
    tok = lambda w: pl.BlockSpec((None, tm, w), lambda b, s: (b, s, 0))
    full = lambda a: pl.BlockSpec(a.shape, lambda b, s: (0,) * a.ndim, pipeline_mode=pl.Buffered(1))
    ttile = pl.BlockSpec((None, None, WIDTH, tm), lambda b, s: (b, s, 0, 0))
    nat = jax.ShapeDtypeStruct((B, S, WIDTH), BF16)
    ttl = jax.ShapeDtypeStruct((B, ns, WIDTH, tm), BF16)
    return pl.pallas_call(
        _inproj_kernel,
        grid=(B, ns),
        in_specs=[tok(D), full(nw), full(wn), full(wt), full(fb),
                  pl.BlockSpec((tm, LANES), lambda b, s: (s, 0)),
                  pl.BlockSpec((tm, LANES), lambda b, s: (s, 0)),
                  pl.BlockSpec((tm, LANES), lambda b, s: (s, 0)),
                  pl.BlockSpec((HALF, tm), lambda b, s: (0, s)),
                  pl.BlockSpec((HALF, tm), lambda b, s: (0, s))],
        out_specs=[tok(WIDTH), tok(WIDTH), tok(WIDTH), tok(WIDTH), ttile, ttile, ttile,
                   pl.BlockSpec((None, None, N_HEADS, tm), lambda b, s: (b, s, 0, 0))],
        out_shape=[nat, nat, nat, nat, ttl, ttl, ttl, jax.ShapeDtypeStruct((B, ns, N_HEADS, tm), F32)],
        scratch_shapes=[pltpu.VMEM((N_HEADS, 1), F32)],
        compiler_params=pltpu.CompilerParams(dimension_semantics=("parallel", "arbitrary"),
                                             vmem_limit_bytes=VMEM_LIMIT),
        name="inproj",
    )(h, nw, wn, wt, fb, cosn, sina, sinb, cost, sint)


def _fox_kernel(k_ref, qt_ref, vt_ref, cum_ref, o_ref, kaug_ref, vaug_ref, st_ref, m_ref, acc_ref):
    nk, _, tk = vt_ref.shape
    frow = lax.broadcasted_iota(jnp.int32, (LANES, tk), 0)
    klane = lax.broadcasted_iota(jnp.int32, (tk, LANES), 1)
    aug0 = (HEAD_DIM, 0)

    def build(j, carry):
        k0 = pl.multiple_of(j * tk, tk)
        kt = k_ref[pl.ds(k0, tk), :].astype(F32)
        vt = vt_ref[j].astype(F32)
        terms = _split3(cum_ref[j] * (-LOG2E))
        for hd in range(2):
            own = (klane < HEAD_DIM) if hd == 0 else (klane >= HEAD_DIM)
            own_t = (frow < HEAD_DIM) if hd == 0 else (frow >= HEAD_DIM)
            a = jnp.zeros((LANES, tk), F32)
            for t in range(N_AUG):
                a = jnp.where(frow == aug0[hd] + t, terms[t][hd:hd + 1], a)
            kaug_ref[hd, pl.ds(k0, tk), :] = (jnp.where(own, kt, 0.0) + a.T).astype(BF16)
            vaug_ref[hd, j] = jnp.where(own_t, vt, 1.0).astype(BF16)
        return carry

    lax.fori_loop(0, nk, build, 0)

    sub = FOX_SUB
    ks_per = tk // sub
    qs_per = FOX_QTILES * ks_per
    tq = FOX_QTILES * tk
    causal_t = lax.broadcasted_iota(jnp.int32, (sub, sub), 0) <= lax.broadcasted_iota(jnp.int32, (sub, sub), 1)
    qrow = lax.broadcasted_iota(jnp.int32, (LANES, tk), 0)
    full_jobs = [(hd, qh, kh) for kh in range(ks_per) for hd in range(2) for qh in range(qs_per)]

    def diag_jobs(d):
        return [(hd, qh, kh) for (hd, qh, kh) in full_jobs if d * ks_per + kh <= qh]

    def q_block(qi, carry):
        qa = []
        for t in range(FOX_QTILES):
            qt = qt_ref[qi * FOX_QTILES + t].astype(F32)
            qa.append([jnp.where(qrow < HEAD_DIM, qt, jnp.where(qrow < HEAD_DIM + N_AUG, 1.0, 0.0)).astype(BF16),
                       jnp.where(qrow >= HEAD_DIM, qt, jnp.where(qrow < N_AUG, 1.0, 0.0)).astype(BF16)])
        m_ref[...] = jnp.full(m_ref.shape, -jnp.inf, F32)
        acc_ref[...] = jnp.zeros(acc_ref.shape, F32)

        def issue_qk(slot, kj, job):
            hd, qh, kh = job
            ks = kaug_ref[hd, pl.ds(pl.multiple_of(kj * tk, tk) + kh * sub, sub), :]
            qsub = qa[qh // ks_per][hd][:, (qh % ks_per) * sub:(qh % ks_per + 1) * sub]
            st_ref[slot, full_jobs.index(job)] = jnp.dot(ks, qsub, preferred_element_type=F32)

        def consume(slot, kj, job, diag):
            hd, qh, kh = job
            st = st_ref[slot, full_jobs.index(job)]
            if diag is not None and diag * ks_per + kh == qh:
                st = jnp.where(causal_t, st, -jnp.inf)
            qs = slice(qh * sub, (qh + 1) * sub)
            m_prev = m_ref[hd, :, qs]
            m_new = jnp.maximum(m_prev, jnp.max(st, axis=0, keepdims=True))
            m_ref[hd, :, qs] = m_new
            alpha = jnp.exp2(m_prev - m_new)
            p = jnp.exp2(st - m_new).astype(BF16)
            vs = vaug_ref[hd, kj, :, kh * sub:(kh + 1) * sub]
            acc_ref[hd, :, qs] = alpha * acc_ref[hd, :, qs] + jnp.dot(vs, p, preferred_element_type=F32)

        def step(slot, kj, jobs, diag, next_jobs):
            for n in range(max(len(jobs), len(next_jobs))):
                if n < len(next_jobs):
                    issue_qk(1 - slot, kj + 1, next_jobs[n])
                if n < len(jobs):
                    consume(slot, kj, jobs[n], diag)

        kbase = qi * FOX_QTILES
        for job in full_jobs:
            issue_qk(0, 0, job)

        def k_body(t, c):
            for r in range(FOX_QTILES):
                step(r % 2, t * FOX_QTILES + r, full_jobs, None, full_jobs)
            return c

        lax.fori_loop(0, qi, k_body, 0)
        for d in range(FOX_QTILES):
            jobs = diag_jobs(d)
            step(d % 2, kbase + d, jobs, d, diag_jobs(d + 1) if d + 1 < FOX_QTILES else [])
        a0, a1 = acc_ref[0], acc_ref[1]
        orow = lax.broadcasted_iota(jnp.int32, a0.shape, 0)
        ot = jnp.where(orow < HEAD_DIM, a0 / a0[HEAD_DIM:HEAD_DIM + 1], a1 / a1[0:1])
        o_ref[pl.ds(pl.multiple_of(qi * tq, tq), tq), :] = ot.T.astype(o_ref.dtype)
        return carry

    lax.fori_loop(0, nk // FOX_QTILES, q_block, 0)


def _fox(fk, fqt, fvt, cum):
    B, S, _ = fk.shape
    _, nk, _, tk = fqt.shape
    cum5 = cum.reshape(B, nk, N_PAIRS, 2, tk)
    assert FOX_QTILES % 2 == 0 and nk % FOX_QTILES == 0 and tk % FOX_SUB == 0
    tq = FOX_QTILES * tk
    n_jobs = (tq // FOX_SUB) * (tk // FOX_SUB)
    nat = pl.BlockSpec((None, S, LANES), lambda b, p: (b, 0, p))
    ttile = pl.BlockSpec((None, nk, LANES, tk), lambda b, p: (b, 0, p, 0))
    return pl.pallas_call(
        _fox_kernel,
        grid=(B, N_PAIRS),
        in_specs=[nat, ttile, ttile,
                  pl.BlockSpec((None, nk, None, 2, tk), lambda b, p: (b, 0, p, 0, 0))],
        out_specs=nat,
        out_shape=jax.ShapeDtypeStruct((B, S, WIDTH), BF16),
        scratch_shapes=[pltpu.VMEM((2, S, LANES), BF16), pltpu.VMEM((2, nk, LANES, tk), BF16),
                        pltpu.VMEM((2, 2 * n_jobs, FOX_SUB, FOX_SUB), F32),
                        pltpu.VMEM((2, 1, tq), F32), pltpu.VMEM((2, LANES, tq), F32)],
        compiler_params=pltpu.CompilerParams(dimension_semantics=("parallel", "parallel"),
                                             vmem_limit_bytes=VMEM_LIMIT),
        name="fox_attention",
    )(fk, fqt, fvt, cum5)


def _ret_kernel(q_ref, kt_ref, v_ref, g_ref, dm_ref, qd_ref, kd_ref, cd_ref, bd_ref, gn_ref, o_ref, st_ref):
    nk, _, tk = kt_ref.shape
    C = dm_ref.shape[-1]
    lane = lax.broadcasted_iota(jnp.int32, (C, LANES), 1)
    st_ref[...] = jnp.zeros_like(st_ref)
    gn = gn_ref[...]

    def group_mean(y):
        y_hi = y.astype(BF16)
        y_lo = (y - y_hi.astype(F32)).astype(BF16)
        return (jnp.dot(y_hi, gn, preferred_element_type=F32) + jnp.dot(y_lo, gn, preferred_element_type=F32))

    def tile_body(j, carry):
        kt_tile = kt_ref[j]
        for sc in range(tk // C):
            r0 = pl.multiple_of(j * tk + sc * C, C)
            q2 = q_ref[pl.ds(r0, C), :]
            v2 = v_ref[pl.ds(r0, C), :]
            kt = kt_tile[:, sc * C:(sc + 1) * C]
            q2f = q2.astype(F32)
            inner = []
            for hd in range(2):
                keep = (lane < HEAD_DIM) if hd == 0 else (lane >= HEAD_DIM)
                qm = jnp.where(keep, q2f, 0.0).astype(BF16)
                sc_h = jnp.dot(qm, kt, preferred_element_type=F32) * dm_ref[hd]
                inner.append(jnp.dot(sc_h.astype(BF16), v2, preferred_element_type=F32))
            st = st_ref[...]
            cross = jnp.dot(q2, st.astype(BF16), preferred_element_type=F32) * qd_ref[...]
            y = jnp.where(lane < HEAD_DIM, inner[0], inner[1]) + cross
            kdec = (kt.astype(F32) * kd_ref[...]).astype(BF16)
            st_ref[...] = st * cd_ref[...] + jnp.dot(kdec, v2, preferred_element_type=F32) * bd_ref[...]
            d = y - group_mean(y)
            var = group_mean(d * d)
            out = d * lax.rsqrt(var + GN_EPS) * g_ref[pl.ds(r0, C), :].astype(F32)
            o_ref[pl.ds(r0, C), :] = out.astype(o_ref.dtype)
        return carry

    lax.fori_loop(0, nk, tile_body, 0)


def _ret(rq, rkt, rv, rg, tabs):
    B, S, _ = rq.shape
    _, nk, _, tk = rkt.shape
    dm, qd, kd, cd, bd, gn = tabs
    C = dm.shape[-1]
    nat = pl.BlockSpec((None, S, LANES), lambda b, p: (b, 0, p))
    return pl.pallas_call(
        _ret_kernel,
        grid=(B, N_PAIRS),
        in_specs=[nat,
                  pl.BlockSpec((None, nk, LANES, tk), lambda b, p: (b, 0, p, 0)),
                  nat, nat,
                  pl.BlockSpec((None, 2, C, C), lambda b, p: (p, 0, 0, 0)),
                  pl.BlockSpec((None, C, LANES), lambda b, p: (p, 0, 0)),
                  pl.BlockSpec((None, LANES, C), lambda b, p: (p, 0, 0)),
                  pl.BlockSpec((None, LANES, LANES), lambda b, p: (p, 0, 0)),
                  pl.BlockSpec((LANES, LANES), lambda b, p: (0, 0)),
                  pl.BlockSpec((LANES, LANES), lambda b, p: (0, 0))],
        out_specs=nat,
        out_shape=jax.ShapeDtypeStruct((B, S, WIDTH), BF16),
        scratch_shapes=[pltpu.VMEM((LANES, LANES), F32)],
        compiler_params=pltpu.CompilerParams(dimension_semantics=("parallel", "parallel"),
                                             vmem_limit_bytes=VMEM_LIMIT),
        name="retention",
    )(rq, rkt, rv, rg, dm, qd, kd, cd, bd, gn)


def _mixer_kernel(h_ref, fox_ref, ret_ref, p_ref, wo_ref, fnw_ref, wa_ref, wg_ref, cw_ref, cb_ref, wd_ref,
                  pnw_ref, wpg_ref, wpp_ref, onw_ref, o_ref, abuf_ref, tail_ref, *, final_norm):
    tm = h_ref.shape[0]
    d_ff = wa_ref.shape[1]
    halo = tail_ref.shape[0]

    @pl.when(pl.program_id(1) == 0)
    def _():
        tail_ref[...] = jnp.zeros_like(tail_ref)

    h1 = (h_ref[...] + jnp.dot(fox_ref[...], wo_ref[0:WIDTH], preferred_element_type=F32)
          + jnp.dot(ret_ref[...], wo_ref[WIDTH:2 * WIDTH], preferred_element_type=F32))
    u = _rms(h1, fnw_ref[...]).astype(BF16)
    down = None
    for c0 in range(0, d_ff, FF_CHUNK):
        c1 = min(c0 + FF_CHUNK, d_ff)
        w = c1 - c0
        a = jnp.dot(u, wa_ref[:, c0:c1], preferred_element_type=F32)
        g = jnp.dot(u, wg_ref[:, c0:c1], preferred_element_type=F32)
        abuf_ref[0:halo, 0:w] = tail_ref[:, c0:c1]
        abuf_ref[halo:halo + tm, 0:w] = a
        tail_ref[:, c0:c1] = a[tm - halo:tm]
        y = cb_ref[:, c0:c1]
        for t in range(CONV_WIDTH):
            off = halo - (CONV_WIDTH - 1) + t
            y = y + abuf_ref[off:off + tm, 0:w] * cw_ref[t:t + 1, c0:c1]
        act = (0.5 * y * (1.0 + lax.erf(y * np.float32(1.0 / np.sqrt(2.0)))) * g).astype(BF16)
        part = jnp.dot(act, wd_ref[c0:c1, :], preferred_element_type=F32)
        down = part if down is None else down + part
    h2 = h1 + down
    n = _rms(h2, pnw_ref[...]).astype(BF16)
    gate = jax.nn.sigmoid(jnp.dot(n, wpg_ref[...], preferred_element_type=F32))
    pe = jnp.dot(p_ref[...].astype(BF16), wpp_ref[...], preferred_element_type=F32)
    h3 = h2 + gate * pe
    if final_norm:
        h3 = _rms(h3, onw_ref[...])
    o_ref[...] = h3


def _mixer(h, fox, ret, p, wo, fnw, wa, wg, cw, cb, wd, pnw, wpg, wpp, onw, tm, final_norm):
    B, S, D = h.shape
    tok = lambda w: pl.BlockSpec((None, tm, w), lambda b, s: (b, s, 0))
    res = lambda a: pl.BlockSpec(a.shape, lambda b, s: (0,) * a.ndim, pipeline_mode=pl.Buffered(1))
    return pl.pallas_call(
        functools.partial(_mixer_kernel, final_norm=final_norm),
        grid=(B, S // tm),
        in_specs=[tok(D), tok(WIDTH), tok(WIDTH), tok(p.shape[-1])]
                 + [res(a) for a in (wo, fnw, wa, wg, cw, cb, wd, pnw, wpg, wpp, onw)],
        out_specs=tok(D),
        out_shape=jax.ShapeDtypeStruct((B, S, D), F32),
        scratch_shapes=[pltpu.VMEM((tm + 8, FF_CHUNK), F32), pltpu.VMEM((8, wa.shape[1]), F32)],
        compiler_params=pltpu.CompilerParams(dimension_semantics=("parallel", "arbitrary"),
                                             vmem_limit_bytes=VMEM_LIMIT),
        name="channel_mixer",
    )(h, fox, ret, p, wo, fnw, wa, wg, cw, cb, wd, pnw, wpg, wpp, onw)


def _rotary_tables(seq):
    inv_freq = ROPE_BASE ** (-jnp.arange(0, HEAD_DIM, 2, dtype=F32) / HEAD_DIM)
    ang = jnp.arange(seq, dtype=F32)[:, None] * inv_freq[None, :]
    cos, sin = jnp.cos(ang), jnp.sin(ang)
    first = (jnp.arange(LANES) % HEAD_DIM) < HALF
    cosn = jnp.tile(cos, (1, LANES // HALF))
    sinn = jnp.tile(sin, (1, LANES // HALF))
    sina = jnp.where(first[None, :], -sinn, 0.0)
    sinb = jnp.where(first[None, :], 0.0, sinn)
    return cosn, sina, sinb, cos.T, sin.T


def _retention_tables(C):
    log_gamma = jnp.log1p(-jnp.exp2(-5.0 - jnp.arange(N_HEADS, dtype=F32)))
    idx = jnp.arange(C, dtype=F32)
    rel = idx[:, None] - idx[None, :]
    lg = log_gamma[:, None, None]
    dm = jnp.where(rel >= 0, jnp.exp(lg * jnp.maximum(rel, 0.0)), 0.0).reshape(N_PAIRS, 2, C, C)
    q_decay = jnp.exp(log_gamma[:, None] * (idx + 1.0))
    k_decay = jnp.exp(log_gamma[:, None] * (C - 1.0 - idx))
    chunk_decay = jnp.exp(log_gamma * C)
    qd = jnp.repeat(q_decay.reshape(N_PAIRS, 2, C), HEAD_DIM, axis=1).transpose(0, 2, 1)
    kd = jnp.repeat(k_decay.reshape(N_PAIRS, 2, C), HEAD_DIM, axis=1)
    blk = jnp.arange(LANES) // HEAD_DIM
    bd = (blk[:, None] == blk[None, :]).astype(F32)
    cd = jnp.repeat(chunk_decay.reshape(N_PAIRS, 2), HEAD_DIM, axis=1)[:, :, None] * bd[None]
    gn = (bd / HEAD_DIM).astype(BF16)
    return dm, qd, kd, cd, bd, gn


def kernel(x, p, attn_norm_w, w_in, forget_bias, w_out, ffn_norm_w, w_up, conv_w, conv_b, w_down, ple_norm_w,
           w_ple_gate, w_ple_proj, final_norm_w):
    B, S, D = x.shape
    depth = w_in.shape[0]
    d_ff = w_down.shape[1]
    tm = min(SEQ_TILE, S)
    assert S % tm == 0 and tm % RET_CHUNK == 0 and D == 2 * WIDTH
    scale = HEAD_DIM ** -0.5
    rot = _rotary_tables(S)
    rtab = _retention_tables(RET_CHUNK)

    o = [0, WIDTH, 2 * WIDTH, 3 * WIDTH, 3 * WIDTH + N_HEADS]
    o += [o[-1] + WIDTH, o[-1] + 2 * WIDTH, o[-1] + 3 * WIDTH, o[-1] + 4 * WIDTH]
    row = lambda v: v.reshape(1, -1)
    h = x
    for i in range(depth):
        wi = w_in[i]
        col = lambda k: wi[:, o[k]:o[k + 1]]
        wn = jnp.concatenate([col(1), col(4), col(6), col(7)], axis=1).astype(BF16)
        wt = jnp.concatenate([col(0) * (scale * LOG2E), col(2), col(5) * scale, col(3),
                              jnp.zeros((D, BF16_ROWS - N_HEADS), F32)], axis=1).T.astype(BF16)
        fk, rq, rv, rg, fqt, fvt, rkt, cum = _inproj(h, row(attn_norm_w[i]), wn, wt,
                                                     forget_bias[i].reshape(N_HEADS, 1), rot, tm)
        fox = _fox(fk, fqt, fvt, cum)
        ret = _ret(rq, rkt, rv, rg, rtab)
        h = _mixer(h, fox, ret, p[i], w_out[i].astype(BF16), row(ffn_norm_w[i]),
                   w_up[i][:, :d_ff].astype(BF16), w_up[i][:, d_ff:].astype(BF16), conv_w[i], row(conv_b[i]),
                   w_down[i].astype(BF16), row(ple_norm_w[i]), w_ple_gate[i].astype(BF16),
                   w_ple_proj[i].astype(BF16), row(final_norm_w), tm, final_norm=(i == depth - 1))
    return h
```

```python
import functools

import numpy as np
import jax
import jax.numpy as jnp
from jax import lax
from jax.experimental import pallas as pl
from jax.experimental.pallas import tpu as pltpu

F32 = jnp.float32
BF16 = jnp.bfloat16

HEAD_DIM = 64
HALF = HEAD_DIM // 2
N_HEADS = 8
WIDTH = N_HEADS * HEAD_DIM
LANES = 128
N_PAIRS = WIDTH // LANES
BF16_ROWS = 16
N_AUG = 3
CONV_WIDTH = 3
ROPE_BASE = 10000.0
NORM_EPS = 1e-6
GN_EPS = 1e-5
LOG2E = float(np.log2(np.e))
SEQ_TILE = 512
FOX_SUB = 256
FOX_QTILES = 4
RET_CHUNK = 256
FF_CHUNK = 1024
VMEM_LIMIT = 60 * 1024 * 1024


def _rms(x, w):
    return x * lax.rsqrt(jnp.mean(x * x, axis=-1, keepdims=True) + NORM_EPS) * w


def _split3(x):
    hi = x.astype(BF16).astype(F32)
    r = x - hi
    mid = r.astype(BF16).astype(F32)
    lo = (r - mid).astype(BF16).astype(F32)
    return hi, mid, lo


def _inproj_kernel(h_ref, nw_ref, wn_ref, wt_ref, fb_ref, cosn_ref, sina_ref, sinb_ref, cost_ref, sint_ref,
                   fk_ref, rq_ref, rv_ref, rg_ref, fqt_ref, fvt_ref, rkt_ref, cum_ref, carry_ref):
    tm = h_ref.shape[0]

    @pl.when(pl.program_id(1) == 0)
    def _():
        carry_ref[...] = jnp.zeros_like(carry_ref)

    u = _rms(h_ref[...], nw_ref[...]).astype(BF16)
    pn = jnp.dot(u, wn_ref[...], preferred_element_type=F32)
    pt = lax.dot_general(wt_ref[...], u, (((1,), (1,)), ((), ())),
                         preferred_element_type=F32)

    fk_ref[...] = pn[:, 0:WIDTH].astype(BF16)
    rv_ref[...] = pn[:, 2 * WIDTH:3 * WIDTH].astype(BF16)
    rg = pn[:, 3 * WIDTH:4 * WIDTH]
    rg_ref[...] = (rg * jax.nn.sigmoid(rg)).astype(BF16)

    cosn, sina, sinb = cosn_ref[...], sina_ref[...], sinb_ref[...]
    for g in range(N_PAIRS):
        xg = pn[:, WIDTH + g * LANES:WIDTH + (g + 1) * LANES]
        o = xg * cosn + pltpu.roll(xg, LANES - HALF, 1) * sina + pltpu.roll(xg, HALF, 1) * sinb
        rq_ref[:, g * LANES:(g + 1) * LANES] = o.astype(BF16)

    fqt_ref[...] = pt[0:WIDTH].astype(BF16)
    fvt_ref[...] = pt[WIDTH:2 * WIDTH].astype(BF16)
    cost, sint = cost_ref[...], sint_ref[...]
    for hd in range(N_HEADS):
        r0 = 2 * WIDTH + hd * HEAD_DIM
        t1, t2 = pt[r0:r0 + HALF], pt[r0 + HALF:r0 + HEAD_DIM]
        rkt_ref[hd * HEAD_DIM:hd * HEAD_DIM + HALF, :] = (t1 * cost - t2 * sint).astype(BF16)
        rkt_ref[hd * HEAD_DIM + HALF:(hd + 1) * HEAD_DIM, :] = (t1 * sint + t2 * cost).astype(BF16)

    logit = pt[3 * WIDTH:3 * WIDTH + N_HEADS] + fb_ref[...]
    logf = jnp.minimum(logit, 0.0) - jnp.log1p(jnp.exp(-jnp.abs(logit)))
    parts = jnp.concatenate(list(_split3(logf)) + [jnp.zeros_like(logf)], axis=0).astype(BF16)
    row = lax.broadcasted_iota(jnp.int32, (tm, tm), 0)
    col = lax.broadcasted_iota(jnp.int32, (tm, tm), 1)
    tri = jnp.where(row <= col, 1.0, 0.0).astype(BF16)
    cs = jnp.dot(parts, tri, preferred_element_type=F32)
    cum_ref[...] = cs[0:8] + cs[8:16] + cs[16:24] + carry_ref[...]
    carry_ref[...] = carry_ref[...] + jnp.sum(logf, axis=1, keepdims=True)


def _inproj(h, nw, wn, wt, fb, tabs, tm):
    B, S, D = h.shape
    ns = S // tm
    cosn, sina, sinb, cost, sint = tabs
    tok = lambda w: pl.BlockSpec((None, tm, w), lambda b, s: (b, s, 0))
    full = lambda a: pl.BlockSpec(a.shape, lambda b, s: (0,) * a.ndim, pipeline_mode=pl.Buffered(1))
    ttile = pl.BlockSpec((None, None, WIDTH, tm), lambda b, s: (b, s, 0, 0))
    nat = jax.ShapeDtypeStruct((B, S, WIDTH), BF16)
    ttl = jax.ShapeDtypeStruct((B, ns, WIDTH, tm), BF16)
    return pl.pallas_call(
        _inproj_kernel,
        grid=(B, ns),
        in_specs=[tok(D), full(nw), full(wn), full(wt), full(fb),
                  pl.BlockSpec((tm, LANES), lambda b, s: (s, 0)),
                  pl.BlockSpec((tm, LANES), lambda b, s: (s, 0)),
                  pl.BlockSpec((tm, LANES), lambda b, s: (s, 0)),
                  pl.BlockSpec((HALF, tm), lambda b, s: (0, s)),
                  pl.BlockSpec((HALF, tm), lambda b, s: (0, s))],
        out_specs=[tok(WIDTH), tok(WIDTH), tok(WIDTH), tok(WIDTH), ttile, ttile, ttile,
                   pl.BlockSpec((None, None, N_HEADS, tm), lambda b, s: (b, s, 0, 0))],
        out_shape=[nat, nat, nat, nat, ttl, ttl, ttl, jax.ShapeDtypeStruct((B, ns, N_HEADS, tm), F32)],
        scratch_shapes=[pltpu.VMEM((N_HEADS, 1), F32)],
        compiler_params=pltpu.CompilerParams(dimension_semantics=("parallel", "arbitrary"),
                                             vmem_limit_bytes=VMEM_LIMIT),
        name="inproj",
    )(h, nw, wn, wt, fb, cosn, sina, sinb, cost, sint)


def _fox_kernel(k_ref, qt_ref, vt_ref, cum_ref, o_ref, kaug_ref, vaug_ref, st_ref, m_ref, acc_ref):
    nk, _, tk = vt_ref.shape
    frow = lax.broadcasted_iota(jnp.int32, (LANES, tk), 0)
    klane = lax.broadcasted_iota(jnp.int32, (tk, LANES), 1)
    aug0 = (HEAD_DIM, 0)

    def build(j, carry):
        k0 = pl.multiple_of(j * tk, tk)
        kt = k_ref[pl.ds(k0, tk), :].astype(F32)
        vt = vt_ref[j].astype(F32)
        terms = _split3(cum_ref[j] * (-LOG2E))
        for hd in range(2):
            own = (klane < HEAD_DIM) if hd == 0 else (klane >= HEAD_DIM)
            own_t = (frow < HEAD_DIM) if hd == 0 else (frow >= HEAD_DIM)
            a = jnp.zeros((LANES, tk), F32)
            for t in range(N_AUG):
                a = jnp.where(frow == aug0[hd] + t, terms[t][hd:hd + 1], a)
            kaug_ref[hd, pl.ds(k0, tk), :] = (jnp.where(own, kt, 0.0) + a.T).astype(BF16)
            vaug_ref[hd, j] = jnp.where(own_t, vt, 1.0).astype(BF16)
        return carry

    lax.fori_loop(0, nk, build, 0)

    sub = FOX_SUB
    ks_per = tk // sub
    qs_per = FOX_QTILES * ks_per
    tq = FOX_QTILES * tk
    qrow = lax.broadcasted_iota(jnp.int32, (LANES, tk), 0)
    full_jobs = [(hd, qh) for qh in range(qs_per) for hd in range(2)]

    def diag_jobs(d):
        return [(hd, qh) for (hd, qh) in full_jobs if d * ks_per <= qh]

    def q_block(qi, carry):
        qa = []
        for t in range(FOX_QTILES):
            qt = qt_ref[qi * FOX_QTILES + t].astype(F32)
            qa.append([jnp.where(qrow < HEAD_DIM, qt, jnp.where(qrow < HEAD_DIM + N_AUG, 1.0, 0.0)).astype(BF16),
                       jnp.where(qrow >= HEAD_DIM, qt, jnp.where(qrow < N_AUG, 1.0, 0.0)).astype(BF16)])
        m_ref[...] = jnp.full(m_ref.shape, -jnp.inf, F32)
        acc_ref[...] = jnp.zeros(acc_ref.shape, F32)

        def issue_qk(slot, kj, job):
            hd, qh = job
            ks = kaug_ref[hd, pl.ds(pl.multiple_of(kj * tk, tk), tk), :]
            qsub = qa[qh // ks_per][hd][:, (qh % ks_per) * sub:(qh % ks_per + 1) * sub]
            st_ref[slot, full_jobs.index(job)] = jnp.dot(ks, qsub, preferred_element_type=F32)

        def consume(slot, kj, job, diag):
            hd, qh = job
            st = st_ref[slot, full_jobs.index(job)]
            if diag is not None and (diag + 1) * ks_per > qh:
                ahead = (lax.broadcasted_iota(jnp.int32, (tk, sub), 0)
                         - lax.broadcasted_iota(jnp.int32, (tk, sub), 1))
                st = jnp.where(ahead <= qh * sub - diag * tk, st, -jnp.inf)
            qs = slice(qh * sub, (qh + 1) * sub)
            m_prev = m_ref[hd, :, qs]
            m_new = jnp.maximum(m_prev, jnp.max(st, axis=0, keepdims=True))
            m_ref[hd, :, qs] = m_new
            alpha = jnp.exp2(m_prev - m_new)
            p = jnp.exp2(st - m_new).astype(BF16)
            acc_ref[hd, :, qs] = alpha * acc_ref[hd, :, qs] + jnp.dot(vaug_ref[hd, kj], p,
                                                                      preferred_element_type=F32)

        def step(slot, kj, jobs, diag, next_jobs):
            for n in range(max(len(jobs), len(next_jobs))):
                if n < len(next_jobs):
                    issue_qk(1 - slot, kj + 1, next_jobs[n])
                if n < len(jobs):
                    consume(slot, kj, jobs[n], diag)

        kbase = qi * FOX_QTILES
        for job in full_jobs:
            issue_qk(0, 0, job)

        def k_body(t, c):
            for r in range(FOX_QTILES):
                step(r % 2, t * FOX_QTILES + r, full_jobs, None, full_jobs)
            return c

        lax.fori_loop(0, qi, k_body, 0)
        for d in range(FOX_QTILES):
            jobs = diag_jobs(d)
            step(d % 2, kbase + d, jobs, d, diag_jobs(d + 1) if d + 1 < FOX_QTILES else [])
        a0, a1 = acc_ref[0], acc_ref[1]
        orow = lax.broadcasted_iota(jnp.int32, a0.shape, 0)
        ot = jnp.where(orow < HEAD_DIM, a0 / a0[HEAD_DIM:HEAD_DIM + 1], a1 / a1[0:1])
        o_ref[pl.ds(pl.multiple_of(qi * tq, tq), tq), :] = ot.T.astype(o_ref.dtype)
        return carry

    lax.fori_loop(0, nk // FOX_QTILES, q_block, 0)


def _fox(fk, fqt, fvt, cum):
    B, S, _ = fk.shape
    _, nk, _, tk = fqt.shape
    cum5 = cum.reshape(B, nk, N_PAIRS, 2, tk)
    assert FOX_QTILES % 2 == 0 and nk % FOX_QTILES == 0 and tk % FOX_SUB == 0
    tq = FOX_QTILES * tk
    n_jobs = tq // FOX_SUB
    nat = pl.BlockSpec((None, S, LANES), lambda b, p: (b, 0, p))
    ttile = pl.BlockSpec((None, nk, LANES, tk), lambda b, p: (b, 0, p, 0))
    return pl.pallas_call(
        _fox_kernel,
        grid=(B, N_PAIRS),
        in_specs=[nat, ttile, ttile,
                  pl.BlockSpec((None, nk, None, 2, tk), lambda b, p: (b, 0, p, 0, 0))],
        out_specs=nat,
        out_shape=jax.ShapeDtypeStruct((B, S, WIDTH), BF16),
        scratch_shapes=[pltpu.VMEM((2, S, LANES), BF16), pltpu.VMEM((2, nk, LANES, tk), BF16),
                        pltpu.VMEM((2, 2 * n_jobs, tk, FOX_SUB), F32),
                        pltpu.VMEM((2, 1, tq), F32), pltpu.VMEM((2, LANES, tq), F32)],
        compiler_params=pltpu.CompilerParams(dimension_semantics=("parallel", "parallel"),
                                             vmem_limit_bytes=VMEM_LIMIT),
        name="fox_attention",
    )(fk, fqt, fvt, cum5)


def _ret_kernel(q_ref, kt_ref, v_ref, g_ref, dm_ref, qd_ref, kd_ref, cd_ref, bd_ref, gn_ref, o_ref, st_ref):
    nk, _, tk = kt_ref.shape
    C = dm_ref.shape[-1]
    lane = lax.broadcasted_iota(jnp.int32, (C, LANES), 1)
    st_ref[...] = jnp.zeros_like(st_ref)
    gn = gn_ref[...]

    def group_mean(y):
        y_hi = y.astype(BF16)
        y_lo = (y - y_hi.astype(F32)).astype(BF16)
        return (jnp.dot(y_hi, gn, preferred_element_type=F32) + jnp.dot(y_lo, gn, preferred_element_type=F32))

    def tile_body(j, carry):
        kt_tile = kt_ref[j]
        for sc in range(tk // C):
            r0 = pl.multiple_of(j * tk + sc * C, C)
            q2 = q_ref[pl.ds(r0, C), :]
            v2 = v_ref[pl.ds(r0, C), :]
            kt = kt_tile[:, sc * C:(sc + 1) * C]
            q2f = q2.astype(F32)
            inner = []
            for hd in range(2):
                keep = (lane < HEAD_DIM) if hd == 0 else (lane >= HEAD_DIM)
                qm = jnp.where(keep, q2f, 0.0).astype(BF16)
                sc_h = jnp.dot(qm, kt, preferred_element_type=F32) * dm_ref[hd]
                inner.append(jnp.dot(sc_h.astype(BF16), v2, preferred_element_type=F32))
            st = st_ref[...]
            cross = jnp.dot(q2, st.astype(BF16), preferred_element_type=F32) * qd_ref[...]
            y = jnp.where(lane < HEAD_DIM, inner[0], inner[1]) + cross
            kdec = (kt.astype(F32) * kd_ref[...]).astype(BF16)
            st_ref[...] = st * cd_ref[...] + jnp.dot(kdec, v2, preferred_element_type=F32) * bd_ref[...]
            d = y - group_mean(y)
            var = group_mean(d * d)
            out = d * lax.rsqrt(var + GN_EPS) * g_ref[pl.ds(r0, C), :].astype(F32)
            o_ref[pl.ds(r0, C), :] = out.astype(o_ref.dtype)
        return carry

    lax.fori_loop(0, nk, tile_body, 0)


def _ret(rq, rkt, rv, rg, tabs):
    B, S, _ = rq.shape
    _, nk, _, tk = rkt.shape
    dm, qd, kd, cd, bd, gn = tabs
    C = dm.shape[-1]
    nat = pl.BlockSpec((None, S, LANES), lambda b, p: (b, 0, p))
    return pl.pallas_call(
        _ret_kernel,
        grid=(B, N_PAIRS),
        in_specs=[nat,
                  pl.BlockSpec((None, nk, LANES, tk), lambda b, p: (b, 0, p, 0)),
                  nat, nat,
                  pl.BlockSpec((None, 2, C, C), lambda b, p: (p, 0, 0, 0)),
                  pl.BlockSpec((None, C, LANES), lambda b, p: (p, 0, 0)),
                  pl.BlockSpec((None, LANES, C), lambda b, p: (p, 0, 0)),
                  pl.BlockSpec((None, LANES, LANES), lambda b, p: (p, 0, 0)),
                  pl.BlockSpec((LANES, LANES), lambda b, p: (0, 0)),
                  pl.BlockSpec((LANES, LANES), lambda b, p: (0, 0))],
        out_specs=nat,
        out_shape=jax.ShapeDtypeStruct((B, S, WIDTH), BF16),
        scratch_shapes=[pltpu.VMEM((LANES, LANES), F32)],
        compiler_params=pltpu.CompilerParams(dimension_semantics=("parallel", "parallel"),
                                             vmem_limit_bytes=VMEM_LIMIT),
        name="retention",
    )(rq, rkt, rv, rg, dm, qd, kd, cd, bd, gn)


def _mixer_kernel(h_ref, fox_ref, ret_ref, p_ref, wo_ref, fnw_ref, wa_ref, wg_ref, cw_ref, cb_ref, wd_ref,
                  pnw_ref, wpg_ref, wpp_ref, onw_ref, o_ref, abuf_ref, tail_ref, *, final_norm):
    tm = h_ref.shape[0]
    d_ff = wa_ref.shape[1]
    halo = tail_ref.shape[0]

    @pl.when(pl.program_id(1) == 0)
    def _():
        tail_ref[...] = jnp.zeros_like(tail_ref)

    h1 = (h_ref[...] + jnp.dot(fox_ref[...], wo_ref[0:WIDTH], preferred_element_type=F32)
          + jnp.dot(ret_ref[...], wo_ref[WIDTH:2 * WIDTH], preferred_element_type=F32))
    u = _rms(h1, fnw_ref[...]).astype(BF16)
    down = None
    for c0 in range(0, d_ff, FF_CHUNK):
        c1 = min(c0 + FF_CHUNK, d_ff)
        w = c1 - c0
        a = jnp.dot(u, wa_ref[:, c0:c1], preferred_element_type=F32)
        g = jnp.dot(u, wg_ref[:, c0:c1], preferred_element_type=F32)
        abuf_ref[0:halo, 0:w] = tail_ref[:, c0:c1]
        abuf_ref[halo:halo + tm, 0:w] = a
        tail_ref[:, c0:c1] = a[tm - halo:tm]
        y = cb_ref[:, c0:c1]
        for t in range(CONV_WIDTH):
            off = halo - (CONV_WIDTH - 1) + t
            y = y + abuf_ref[off:off + tm, 0:w] * cw_ref[t:t + 1, c0:c1]
        act = (0.5 * y * (1.0 + lax.erf(y * np.float32(1.0 / np.sqrt(2.0)))) * g).astype(BF16)
        part = jnp.dot(act, wd_ref[c0:c1, :], preferred_element_type=F32)
        down = part if down is None else down + part
    h2 = h1 + down
    n = _rms(h2, pnw_ref[...]).astype(BF16)
    gate = jax.nn.sigmoid(jnp.dot(n, wpg_ref[...], preferred_element_type=F32))
    pe = jnp.dot(p_ref[...].astype(BF16), wpp_ref[...], preferred_element_type=F32)
    h3 = h2 + gate * pe
    if final_norm:
        h3 = _rms(h3, onw_ref[...])
    o_ref[...] = h3


def _mixer(h, fox, ret, p, wo, fnw, wa, wg, cw, cb, wd, pnw, wpg, wpp, onw, tm, final_norm):
    B, S, D = h.shape
    tok = lambda w: pl.BlockSpec((None, tm, w), lambda b, s: (b, s, 0))
    res = lambda a: pl.BlockSpec(a.shape, lambda b, s: (0,) * a.ndim, pipeline_mode=pl.Buffered(1))
    return pl.pallas_call(
        functools.partial(_mixer_kernel, final_norm=final_norm),
        grid=(B, S // tm),
        in_specs=[tok(D), tok(WIDTH), tok(WIDTH), tok(p.shape[-1])]
                 + [res(a) for a in (wo, fnw, wa, wg, cw, cb, wd, pnw, wpg, wpp, onw)],
        out_specs=tok(D),
        out_shape=jax.ShapeDtypeStruct((B, S, D), F32),
        scratch_shapes=[pltpu.VMEM((tm + 8, FF_CHUNK), F32), pltpu.VMEM((8, wa.shape[1]), F32)],
        compiler_params=pltpu.CompilerParams(dimension_semantics=("parallel", "arbitrary"),
                                             vmem_limit_bytes=VMEM_LIMIT),
        name="channel_mixer",
    )(h, fox, ret, p, wo, fnw, wa, wg, cw, cb, wd, pnw, wpg, wpp, onw)


def _rotary_tables(seq):
    inv_freq = ROPE_BASE ** (-jnp.arange(0, HEAD_DIM, 2, dtype=F32) / HEAD_DIM)
    ang = jnp.arange(seq, dtype=F32)[:, None] * inv_freq[None, :]
    cos, sin = jnp.cos(ang), jnp.sin(ang)
    first = (jnp.arange(LANES) % HEAD_DIM) < HALF
    cosn = jnp.tile(cos, (1, LANES // HALF))
    sinn = jnp.tile(sin, (1, LANES // HALF))
    sina = jnp.where(first[None, :], -sinn, 0.0)
    sinb = jnp.where(first[None, :], 0.0, sinn)
    return cosn, sina, sinb, cos.T, sin.T


def _retention_tables(C):
    log_gamma = jnp.log1p(-jnp.exp2(-5.0 - jnp.arange(N_HEADS, dtype=F32)))
    idx = jnp.arange(C, dtype=F32)
    rel = idx[:, None] - idx[None, :]
    lg = log_gamma[:, None, None]
    dm = jnp.where(rel >= 0, jnp.exp(lg * jnp.maximum(rel, 0.0)), 0.0).reshape(N_PAIRS, 2, C, C)
    q_decay = jnp.exp(log_gamma[:, None] * (idx + 1.0))
    k_decay = jnp.exp(log_gamma[:, None] * (C - 1.0 - idx))
    chunk_decay = jnp.exp(log_gamma * C)
    qd = jnp.repeat(q_decay.reshape(N_PAIRS, 2, C), HEAD_DIM, axis=1).transpose(0, 2, 1)
    kd = jnp.repeat(k_decay.reshape(N_PAIRS, 2, C), HEAD_DIM, axis=1)
    blk = jnp.arange(LANES) // HEAD_DIM
    bd = (blk[:, None] == blk[None, :]).astype(F32)
    cd = jnp.repeat(chunk_decay.reshape(N_PAIRS, 2), HEAD_DIM, axis=1)[:, :, None] * bd[None]
    gn = (bd / HEAD_DIM).astype(BF16)
    return dm, qd, kd, cd, bd, gn


def kernel(x, p, attn_norm_w, w_in, forget_bias, w_out, ffn_norm_w, w_up, conv_w, conv_b, w_down, ple_norm_w,
           w_ple_gate, w_ple_proj, final_norm_w):
    B, S, D = x.shape
    depth = w_in.shape[0]
    d_ff = w_down.shape[1]
    tm = min(SEQ_TILE, S)
    assert S % tm == 0 and tm % RET_CHUNK == 0 and D == 2 * WIDTH
    scale = HEAD_DIM ** -0.5
    rot = _rotary_tables(S)
    rtab = _retention_tables(RET_CHUNK)

    o = [0, WIDTH, 2 * WIDTH, 3 * WIDTH, 3 * WIDTH + N_HEADS]
    o += [o[-1] + WIDTH, o[-1] + 2 * WIDTH, o[-1] + 3 * WIDTH, o[-1] + 4 * WIDTH]
    row = lambda v: v.reshape(1, -1)
    h = x
    for i in range(depth):
        wi = w_in[i]
        col = lambda k: wi[:, o[k]:o[k + 1]]
        wn = jnp.concatenate([col(1), col(4), col(6), col(7)], axis=1).astype(BF16)
        wt = jnp.concatenate([col(0) * (scale * LOG2E), col(2), col(5) * scale, col(3),
                              jnp.zeros((D, BF16_ROWS - N_HEADS), F32)], axis=1).T.astype(BF16)
        fk, rq, rv, rg, fqt, fvt, rkt, cum = _inproj(h, row(attn_norm_w[i]), wn, wt,
                                                     forget_bias[i].reshape(N_HEADS, 1), rot, tm)
        fox = _fox(fk, fqt, fvt, cum)
        ret = _ret(rq, rkt, rv, rg, rtab)
        h = _mixer(h, fox, ret, p[i], w_out[i].astype(BF16), row(ffn_norm_w[i]),
                   w_up[i][:, :d_ff].astype(BF16), w_up[i][:, d_ff:].astype(BF16), conv_w[i], row(conv_b[i]),
                   w_down[i].astype(BF16), row(ple_norm_w[i]), w_ple_gate[i].astype(BF16),
                   w_ple_proj[i].astype(BF16), row(final_norm_w), tm, final_norm=(i == depth - 1))
    return h
```

```python
import functools

import numpy as np
import jax
import jax.numpy as jnp
from jax import lax
from jax.experimental import pallas as pl
from jax.experimental.pallas import tpu as pltpu

F32 = jnp.float32
BF16 = jnp.bfloat16

HEAD_DIM = 64
HALF = HEAD_DIM // 2
N_HEADS = 8
WIDTH = N_HEADS * HEAD_DIM
LANES = 128
N_PAIRS = WIDTH // LANES
BF16_ROWS = 16
N_AUG = 3
CONV_WIDTH = 3
ROPE_BASE = 10000.0
NORM_EPS = 1e-6
GN_EPS = 1e-5
LOG2E = float(np.log2(np.e))
SEQ_TILE = 512
FOX_SUB = 256
FOX_QTILES = 4
RET_CHUNK = 256
RET_TILES = 2
MIX_ROWS = 256
FF_CHUNK = 1024
VMEM_LIMIT = 60 * 1024 * 1024


def _rms(x, w):
    return x * lax.rsqrt(jnp.mean(x * x, axis=-1, keepdims=True) + NORM_EPS) * w


def _split3(x):
    hi = x.astype(BF16).astype(F32)
    r = x - hi
    mid = r.astype(BF16).astype(F32)
    lo = (r - mid).astype(BF16).astype(F32)
    return hi, mid, lo


def _inproj_kernel(h_ref, nw_ref, wn_ref, wt_ref, fb_ref, cosn_ref, sina_ref, sinb_ref, cost_ref, sint_ref,
                   fk_ref, rq_ref, rv_ref, rg_ref, fqt_ref, fvt_ref, rkt_ref, cum_ref, carry_ref):
    tm = h_ref.shape[0]

    @pl.when(pl.program_id(1) == 0)
    def _():
        carry_ref[...] = jnp.zeros_like(carry_ref)

    u = _rms(h_ref[...], nw_ref[...]).astype(BF16)
    pn = jnp.dot(u, wn_ref[...], preferred_element_type=F32)
    pt = lax.dot_general(wt_ref[...], u, (((1,), (1,)), ((), ())),
                         preferred_element_type=F32)

    fk_ref[...] = pn[:, 0:WIDTH].astype(BF16)
    rv_ref[...] = pn[:, 2 * WIDTH:3 * WIDTH].astype(BF16)
    rg = pn[:, 3 * WIDTH:4 * WIDTH]
    rg_ref[...] = (rg * jax.nn.sigmoid(rg)).astype(BF16)

    cosn, sina, sinb = cosn_ref[...], sina_ref[...], sinb_ref[...]
    for g in range(N_PAIRS):
        xg = pn[:, WIDTH + g * LANES:WIDTH + (g + 1) * LANES]
        o = xg * cosn + pltpu.roll(xg, LANES - HALF, 1) * sina + pltpu.roll(xg, HALF, 1) * sinb
        rq_ref[:, g * LANES:(g + 1) * LANES] = o.astype(BF16)

    fqt_ref[...] = pt[0:WIDTH].astype(BF16)
    fvt_ref[...] = pt[WIDTH:2 * WIDTH].astype(BF16)
    cost, sint = cost_ref[...], sint_ref[...]
    for hd in range(N_HEADS):
        r0 = 2 * WIDTH + hd * HEAD_DIM
        t1, t2 = pt[r0:r0 + HALF], pt[r0 + HALF:r0 + HEAD_DIM]
        rkt_ref[hd * HEAD_DIM:hd * HEAD_DIM + HALF, :] = (t1 * cost - t2 * sint).astype(BF16)
        rkt_ref[hd * HEAD_DIM + HALF:(hd + 1) * HEAD_DIM, :] = (t1 * sint + t2 * cost).astype(BF16)

    logit = pt[3 * WIDTH:3 * WIDTH + N_HEADS] + fb_ref[...]
    logf = jnp.minimum(logit, 0.0) - jnp.log1p(jnp.exp(-jnp.abs(logit)))
    parts = jnp.concatenate(list(_split3(logf)) + [jnp.zeros_like(logf)], axis=0).astype(BF16)
    row = lax.broadcasted_iota(jnp.int32, (tm, tm), 0)
    col = lax.broadcasted_iota(jnp.int32, (tm, tm), 1)
    tri = jnp.where(row <= col, 1.0, 0.0).astype(BF16)
    cs = jnp.dot(parts, tri, preferred_element_type=F32)
    cum_ref[...] = cs[0:8] + cs[8:16] + cs[16:24] + carry_ref[...]
    carry_ref[...] = carry_ref[...] + jnp.sum(logf, axis=1, keepdims=True)


def _inproj(h, nw, wn, wt, fb, tabs, tm):
    B, S, D = h.shape
    ns = S // tm
    cosn, sina, sinb, cost, sint = tabs
    tok = lambda w: pl.BlockSpec((None, tm, w), lambda b, s: (b, s, 0))
    full = lambda a: pl.BlockSpec(a.shape, lambda b, s: (0,) * a.ndim, pipeline_mode=pl.Buffered(1))
    ttile = pl.BlockSpec((None, None, WIDTH, tm), lambda b, s: (b, s, 0, 0))
    nat = jax.ShapeDtypeStruct((B, S, WIDTH), BF16)
    ttl = jax.ShapeDtypeStruct((B, ns, WIDTH, tm), BF16)
    return pl.pallas_call(
        _inproj_kernel,
        grid=(B, ns),
        in_specs=[tok(D), full(nw), full(wn), full(wt), full(fb),
                  pl.BlockSpec((tm, LANES), lambda b, s: (s, 0)),
                  pl.BlockSpec((tm, LANES), lambda b, s: (s, 0)),
                  pl.BlockSpec((tm, LANES), lambda b, s: (s, 0)),
                  pl.BlockSpec((HALF, tm), lambda b, s: (0, s)),
                  pl.BlockSpec((HALF, tm), lambda b, s: (0, s))],
        out_specs=[tok(WIDTH), tok(WIDTH), tok(WIDTH), tok(WIDTH), ttile, ttile, ttile,
                   pl.BlockSpec((None, None, N_HEADS, tm), lambda b, s: (b, s, 0, 0))],
        out_shape=[nat, nat, nat, nat, ttl, ttl, ttl, jax.ShapeDtypeStruct((B, ns, N_HEADS, tm), F32)],
        scratch_shapes=[pltpu.VMEM((N_HEADS, 1), F32)],
        compiler_params=pltpu.CompilerParams(dimension_semantics=("parallel", "arbitrary"),
                                             vmem_limit_bytes=VMEM_LIMIT),
        name="inproj",
    )(h, nw, wn, wt, fb, cosn, sina, sinb, cost, sint)


def _fox_kernel(k_ref, qt_ref, vt_ref, cum_ref, o_ref, kaug_ref, vaug_ref, st_ref, m_ref, acc_ref):
    nk, _, tk = vt_ref.shape
    frow = lax.broadcasted_iota(jnp.int32, (LANES, tk), 0)
    klane = lax.broadcasted_iota(jnp.int32, (tk, LANES), 1)
    aug0 = (HEAD_DIM, 0)

    def build(j, carry):
        k0 = pl.multiple_of(j * tk, tk)
        kt = k_ref[pl.ds(k0, tk), :].astype(F32)
        vt = vt_ref[j].astype(F32)
        terms = _split3(cum_ref[j] * (-LOG2E))
        for hd in range(2):
            own = (klane < HEAD_DIM) if hd == 0 else (klane >= HEAD_DIM)
            own_t = (frow < HEAD_DIM) if hd == 0 else (frow >= HEAD_DIM)
            a = jnp.zeros((LANES, tk), F32)
            for t in range(N_AUG):
                a = jnp.where(frow == aug0[hd] + t, terms[t][hd:hd + 1], a)
            kaug_ref[hd, pl.ds(k0, tk), :] = (jnp.where(own, kt, 0.0) + a.T).astype(BF16)
            vaug_ref[hd, j] = jnp.where(own_t, vt, 1.0).astype(BF16)
        return carry

    lax.fori_loop(0, nk, build, 0)

    sub = FOX_SUB
    ks_per = tk // sub
    qs_per = FOX_QTILES * ks_per
    tq = FOX_QTILES * tk
    qrow = lax.broadcasted_iota(jnp.int32, (LANES, tk), 0)
    full_jobs = [(hd, qh) for qh in range(qs_per) for hd in range(2)]

    def diag_jobs(d):
        return [(hd, qh) for (hd, qh) in full_jobs if d * ks_per <= qh]

    def q_block(qi, carry):
        qa = []
        for t in range(FOX_QTILES):
            qt = qt_ref[qi * FOX_QTILES + t].astype(F32)
            qa.append([jnp.where(qrow < HEAD_DIM, qt, jnp.where(qrow < HEAD_DIM + N_AUG, 1.0, 0.0)).astype(BF16),
                       jnp.where(qrow >= HEAD_DIM, qt, jnp.where(qrow < N_AUG, 1.0, 0.0)).astype(BF16)])
        m_ref[...] = jnp.full(m_ref.shape, -jnp.inf, F32)
        acc_ref[...] = jnp.zeros(acc_ref.shape, F32)

        def issue_qk(slot, kj, job):
            hd, qh = job
            ks = kaug_ref[hd, pl.ds(pl.multiple_of(kj * tk, tk), tk), :]
            qsub = qa[qh // ks_per][hd][:, (qh % ks_per) * sub:(qh % ks_per + 1) * sub]
            st_ref[slot, full_jobs.index(job)] = jnp.dot(ks, qsub, preferred_element_type=F32)

        def consume(slot, kj, job, diag):
            hd, qh = job
            st = st_ref[slot, full_jobs.index(job)]
            if diag is not None and (diag + 1) * ks_per > qh:
                ahead = (lax.broadcasted_iota(jnp.int32, (tk, sub), 0)
                         - lax.broadcasted_iota(jnp.int32, (tk, sub), 1))
                st = jnp.where(ahead <= qh * sub - diag * tk, st, -jnp.inf)
            qs = slice(qh * sub, (qh + 1) * sub)
            m_prev = m_ref[hd, :, qs]
            m_new = jnp.maximum(m_prev, jnp.max(st, axis=0, keepdims=True))
            m_ref[hd, :, qs] = m_new
            alpha = jnp.exp2(m_prev - m_new)
            p = jnp.exp2(st - m_new).astype(BF16)
            acc_ref[hd, :, qs] = alpha * acc_ref[hd, :, qs] + jnp.dot(vaug_ref[hd, kj], p,
                                                                      preferred_element_type=F32)

        def step(slot, kj, jobs, diag, next_jobs):
            for n in range(max(len(jobs), len(next_jobs))):
                if n < len(next_jobs):
                    issue_qk(1 - slot, kj + 1, next_jobs[n])
                if n < len(jobs):
                    consume(slot, kj, jobs[n], diag)

        kbase = qi * FOX_QTILES
        for job in full_jobs:
            issue_qk(0, 0, job)

        def k_body(t, c):
            for r in range(FOX_QTILES):
                step(r % 2, t * FOX_QTILES + r, full_jobs, None, full_jobs)
            return c

        lax.fori_loop(0, qi, k_body, 0)
        for d in range(FOX_QTILES):
            jobs = diag_jobs(d)
            step(d % 2, kbase + d, jobs, d, diag_jobs(d + 1) if d + 1 < FOX_QTILES else [])
        a0, a1 = acc_ref[0], acc_ref[1]
        orow = lax.broadcasted_iota(jnp.int32, a0.shape, 0)
        ot = jnp.where(orow < HEAD_DIM, a0 / a0[HEAD_DIM:HEAD_DIM + 1], a1 / a1[0:1])
        o_ref[pl.ds(pl.multiple_of(qi * tq, tq), tq), :] = ot.T.astype(o_ref.dtype)
        return carry

    lax.fori_loop(0, nk // FOX_QTILES, q_block, 0)


def _fox(fk, fqt, fvt, cum):
    B, S, _ = fk.shape
    _, nk, _, tk = fqt.shape
    cum5 = cum.reshape(B, nk, N_PAIRS, 2, tk)
    assert FOX_QTILES % 2 == 0 and nk % FOX_QTILES == 0 and tk % FOX_SUB == 0
    tq = FOX_QTILES * tk
    n_jobs = tq // FOX_SUB
    nat = pl.BlockSpec((None, S, LANES), lambda b, p: (b, 0, p))
    ttile = pl.BlockSpec((None, nk, LANES, tk), lambda b, p: (b, 0, p, 0))
    return pl.pallas_call(
        _fox_kernel,
        grid=(B, N_PAIRS),
        in_specs=[nat, ttile, ttile,
                  pl.BlockSpec((None, nk, None, 2, tk), lambda b, p: (b, 0, p, 0, 0))],
        out_specs=nat,
        out_shape=jax.ShapeDtypeStruct((B, S, WIDTH), BF16),
        scratch_shapes=[pltpu.VMEM((2, S, LANES), BF16), pltpu.VMEM((2, nk, LANES, tk), BF16),
                        pltpu.VMEM((2, 2 * n_jobs, tk, FOX_SUB), F32),
                        pltpu.VMEM((2, 1, tq), F32), pltpu.VMEM((2, LANES, tq), F32)],
        compiler_params=pltpu.CompilerParams(dimension_semantics=("parallel", "parallel"),
                                             vmem_limit_bytes=VMEM_LIMIT),
        name="fox_attention",
    )(fk, fqt, fvt, cum5)


def _ret_kernel(q_ref, kt_ref, v_ref, g_ref, dm_ref, qd_ref, kd_ref, cd_ref, bd_ref, gn_ref, o_ref, st_ref):
    nk, _, tk = kt_ref.shape
    C = dm_ref.shape[-1]
    lane = lax.broadcasted_iota(jnp.int32, (C, LANES), 1)
    st_ref[...] = jnp.zeros_like(st_ref)
    gn = gn_ref[...]

    def group_mean(y):
        return jnp.dot(y.astype(BF16), gn, preferred_element_type=F32)

    def trip(j, carry):
        chunks = []
        for t in range(RET_TILES):
            kt_tile = kt_ref[j * RET_TILES + t]
            for sc in range(tk // C):
                r0 = pl.multiple_of((j * RET_TILES + t) * tk + sc * C, C)
                chunks.append((r0, q_ref[pl.ds(r0, C), :], v_ref[pl.ds(r0, C), :],
                               kt_tile[:, sc * C:(sc + 1) * C]))
        scores, kvs = [], []
        for r0, q2, v2, kt in chunks:
            q2f = q2.astype(F32)
            sc_pair = []
            for hd in range(2):
                keep = (lane < HEAD_DIM) if hd == 0 else (lane >= HEAD_DIM)
                qm = jnp.where(keep, q2f, 0.0).astype(BF16)
                sc_pair.append((jnp.dot(qm, kt, preferred_element_type=F32) * dm_ref[hd]).astype(BF16))
            scores.append(sc_pair)
            kdec = (kt.astype(F32) * kd_ref[...]).astype(BF16)
            kvs.append(jnp.dot(kdec, v2, preferred_element_type=F32) * bd_ref[...])
        st = st_ref[...]
        crosses = []
        for (r0, q2, v2, kt), kv in zip(chunks, kvs):
            crosses.append(jnp.dot(q2, st.astype(BF16), preferred_element_type=F32) * qd_ref[...])
            st = st * cd_ref[...] + kv
        st_ref[...] = st
        ys = []
        for (r0, q2, v2, kt), sc_pair, cross in zip(chunks, scores, crosses):
            inner = [jnp.dot(s, v2, preferred_element_type=F32) for s in sc_pair]
            ys.append(jnp.where(lane < HEAD_DIM, inner[0], inner[1]) + cross)
        ds = [y - group_mean(y) for y in ys]
        vs = [group_mean(d * d) for d in ds]
        for (r0, q2, v2, kt), d, var in zip(chunks, ds, vs):
            out = d * lax.rsqrt(var + GN_EPS) * g_ref[pl.ds(r0, C), :].astype(F32)
            o_ref[pl.ds(r0, C), :] = out.astype(o_ref.dtype)
        return carry

    lax.fori_loop(0, nk // RET_TILES, trip, 0)


def _ret(rq, rkt, rv, rg, tabs):
    B, S, _ = rq.shape
    _, nk, _, tk = rkt.shape
    dm, qd, kd, cd, bd, gn = tabs
    C = dm.shape[-1]
    assert nk % RET_TILES == 0 and tk % C == 0
    nat = pl.BlockSpec((None, S, LANES), lambda b, p: (b, 0, p))
    return pl.pallas_call(
        _ret_kernel,
        grid=(B, N_PAIRS),
        in_specs=[nat,
                  pl.BlockSpec((None, nk, LANES, tk), lambda b, p: (b, 0, p, 0)),
                  nat, nat,
                  pl.BlockSpec((None, 2, C, C), lambda b, p: (p, 0, 0, 0)),
                  pl.BlockSpec((None, C, LANES), lambda b, p: (p, 0, 0)),
                  pl.BlockSpec((None, LANES, C), lambda b, p: (p, 0, 0)),
                  pl.BlockSpec((None, LANES, LANES), lambda b, p: (p, 0, 0)),
                  pl.BlockSpec((LANES, LANES), lambda b, p: (0, 0)),
                  pl.BlockSpec((LANES, LANES), lambda b, p: (0, 0))],
        out_specs=nat,
        out_shape=jax.ShapeDtypeStruct((B, S, WIDTH), BF16),
        scratch_shapes=[pltpu.VMEM((LANES, LANES), F32)],
        compiler_params=pltpu.CompilerParams(dimension_semantics=("parallel", "parallel"),
                                             vmem_limit_bytes=VMEM_LIMIT),
        name="retention",
    )(rq, rkt, rv, rg, dm, qd, kd, cd, bd, gn)


def _mixer_kernel(h_ref, fox_ref, ret_ref, p_ref, wo_ref, fnw_ref, wa_ref, wg_ref, cw_ref, cb_ref, wd_ref,
                  pnw_ref, wpg_ref, wpp_ref, onw_ref, o_ref, abuf_ref, tail_ref, *, final_norm):
    tm = h_ref.shape[0]
    d_ff = wa_ref.shape[1]
    halo = tail_ref.shape[0]

    @pl.when(pl.program_id(1) == 0)
    def _():
        tail_ref[...] = jnp.zeros_like(tail_ref)

    strands = [slice(r, r + MIX_ROWS) for r in range(0, tm, MIX_ROWS)]
    h1s, us = [], []
    for rs in strands:
        h1 = (h_ref[rs, :] + jnp.dot(fox_ref[rs, :], wo_ref[0:WIDTH], preferred_element_type=F32)
              + jnp.dot(ret_ref[rs, :], wo_ref[WIDTH:2 * WIDTH], preferred_element_type=F32))
        h1s.append(h1)
        us.append(_rms(h1, fnw_ref[...]).astype(BF16))
    downs = [None] * len(strands)
    for c0 in range(0, d_ff, FF_CHUNK):
        c1 = min(c0 + FF_CHUNK, d_ff)
        w = c1 - c0
        abuf = abuf_ref.at[(c0 // FF_CHUNK) % 2]
        abuf[0:halo, 0:w] = tail_ref[:, c0:c1]
        gs = []
        for i, rs in enumerate(strands):
            a = jnp.dot(us[i], wa_ref[:, c0:c1], preferred_element_type=F32)
            gs.append(jnp.dot(us[i], wg_ref[:, c0:c1], preferred_element_type=F32))
            abuf[halo + rs.start:halo + rs.stop, 0:w] = a
            if rs.stop == tm:
                tail_ref[:, c0:c1] = a[MIX_ROWS - halo:MIX_ROWS]
        for i, rs in enumerate(strands):
            y = cb_ref[:, c0:c1]
            for t in range(CONV_WIDTH):
                off = halo - (CONV_WIDTH - 1) + t + rs.start
                y = y + abuf[off:off + MIX_ROWS, 0:w] * cw_ref[t:t + 1, c0:c1]
            act = (0.5 * y * (1.0 + lax.erf(y * np.float32(1.0 / np.sqrt(2.0)))) * gs[i]).astype(BF16)
            part = jnp.dot(act, wd_ref[c0:c1, :], preferred_element_type=F32)
            downs[i] = part if downs[i] is None else downs[i] + part
    for i, rs in enumerate(strands):
        h2 = h1s[i] + downs[i]
        n = _rms(h2, pnw_ref[...]).astype(BF16)
        gate = jax.nn.sigmoid(jnp.dot(n, wpg_ref[...], preferred_element_type=F32))
        pe = jnp.dot(p_ref[rs, :].astype(BF16), wpp_ref[...], preferred_element_type=F32)
        h3 = h2 + gate * pe
        if final_norm:
            h3 = _rms(h3, onw_ref[...])
        o_ref[rs, :] = h3


def _mixer(h, fox, ret, p, wo, fnw, wa, wg, cw, cb, wd, pnw, wpg, wpp, onw, tm, final_norm):
    B, S, D = h.shape
    tok = lambda w: pl.BlockSpec((None, tm, w), lambda b, s: (b, s, 0))
    res = lambda a: pl.BlockSpec(a.shape, lambda b, s: (0,) * a.ndim, pipeline_mode=pl.Buffered(1))
    return pl.pallas_call(
        functools.partial(_mixer_kernel, final_norm=final_norm),
        grid=(B, S // tm),
        in_specs=[tok(D), tok(WIDTH), tok(WIDTH), tok(p.shape[-1])]
                 + [res(a) for a in (wo, fnw, wa, wg, cw, cb, wd, pnw, wpg, wpp, onw)],
        out_specs=tok(D),
        out_shape=jax.ShapeDtypeStruct((B, S, D), F32),
        scratch_shapes=[pltpu.VMEM((2, tm + 8, FF_CHUNK), F32), pltpu.VMEM((8, wa.shape[1]), F32)],
        compiler_params=pltpu.CompilerParams(dimension_semantics=("parallel", "arbitrary"),
                                             vmem_limit_bytes=VMEM_LIMIT),
        name="channel_mixer",
    )(h, fox, ret, p, wo, fnw, wa, wg, cw, cb, wd, pnw, wpg, wpp, onw)


def _rotary_tables(seq):
    inv_freq = ROPE_BASE ** (-jnp.arange(0, HEAD_DIM, 2, dtype=F32) / HEAD_DIM)
    ang = jnp.arange(seq, dtype=F32)[:, None] * inv_freq[None, :]
    cos, sin = jnp.cos(ang), jnp.sin(ang)
    first = (jnp.arange(LANES) % HEAD_DIM) < HALF
    cosn = jnp.tile(cos, (1, LANES // HALF))
    sinn = jnp.tile(sin, (1, LANES // HALF))
    sina = jnp.where(first[None, :], -sinn, 0.0)
    sinb = jnp.where(first[None, :], 0.0, sinn)
    return cosn, sina, sinb, cos.T, sin.T


def _retention_tables(C):
    log_gamma = jnp.log1p(-jnp.exp2(-5.0 - jnp.arange(N_HEADS, dtype=F32)))
    idx = jnp.arange(C, dtype=F32)
    rel = idx[:, None] - idx[None, :]
    lg = log_gamma[:, None, None]
    dm = jnp.where(rel >= 0, jnp.exp(lg * jnp.maximum(rel, 0.0)), 0.0).reshape(N_PAIRS, 2, C, C)
    q_decay = jnp.exp(log_gamma[:, None] * (idx + 1.0))
    k_decay = jnp.exp(log_gamma[:, None] * (C - 1.0 - idx))
    chunk_decay = jnp.exp(log_gamma * C)
    qd = jnp.repeat(q_decay.reshape(N_PAIRS, 2, C), HEAD_DIM, axis=1).transpose(0, 2, 1)
    kd = jnp.repeat(k_decay.reshape(N_PAIRS, 2, C), HEAD_DIM, axis=1)
    blk = jnp.arange(LANES) // HEAD_DIM
    bd = (blk[:, None] == blk[None, :]).astype(F32)
    cd = jnp.repeat(chunk_decay.reshape(N_PAIRS, 2), HEAD_DIM, axis=1)[:, :, None] * bd[None]
    gn = (bd / HEAD_DIM).astype(BF16)
    return dm, qd, kd, cd, bd, gn


def kernel(x, p, attn_norm_w, w_in, forget_bias, w_out, ffn_norm_w, w_up, conv_w, conv_b, w_down, ple_norm_w,
           w_ple_gate, w_ple_proj, final_norm_w):
    B, S, D = x.shape
    depth = w_in.shape[0]
    d_ff = w_down.shape[1]
    tm = min(SEQ_TILE, S)
    assert S % tm == 0 and tm % RET_CHUNK == 0 and D == 2 * WIDTH
    scale = HEAD_DIM ** -0.5
    rot = _rotary_tables(S)
    rtab = _retention_tables(RET_CHUNK)

    o = [0, WIDTH, 2 * WIDTH, 3 * WIDTH, 3 * WIDTH + N_HEADS]
    o += [o[-1] + WIDTH, o[-1] + 2 * WIDTH, o[-1] + 3 * WIDTH, o[-1] + 4 * WIDTH]
    row = lambda v: v.reshape(1, -1)
    h = x
    for i in range(depth):
        wi = w_in[i]
        col = lambda k: wi[:, o[k]:o[k + 1]]
        wn = jnp.concatenate([col(1), col(4), col(6), col(7)], axis=1).astype(BF16)
        wt = jnp.concatenate([col(0) * (scale * LOG2E), col(2), col(5) * scale, col(3),
                              jnp.zeros((D, BF16_ROWS - N_HEADS), F32)], axis=1).T.astype(BF16)
        fk, rq, rv, rg, fqt, fvt, rkt, cum = _inproj(h, row(attn_norm_w[i]), wn, wt,
                                                     forget_bias[i].reshape(N_HEADS, 1), rot, tm)
        fox = _fox(fk, fqt, fvt, cum)
        ret = _ret(rq, rkt, rv, rg, rtab)
        h = _mixer(h, fox, ret, p[i], w_out[i].astype(BF16), row(ffn_norm_w[i]),
                   w_up[i][:, :d_ff].astype(BF16), w_up[i][:, d_ff:].astype(BF16), conv_w[i], row(conv_b[i]),
                   w_down[i].astype(BF16), row(ple_norm_w[i]), w_ple_gate[i].astype(BF16),
                   w_ple_proj[i].astype(BF16), row(final_norm_w), tm, final_norm=(i == depth - 1))
    return h
```

```python
import functools

import numpy as np
import jax
import jax.numpy as jnp
from jax import lax
from jax.experimental import pallas as pl
from jax.experimental.pallas import tpu as pltpu

F32 = jnp.float32
BF16 = jnp.bfloat16

HEAD_DIM = 64
HALF = HEAD_DIM // 2
N_HEADS = 8
WIDTH = N_HEADS * HEAD_DIM
LANES = 128
N_PAIRS = WIDTH // LANES
BF16_ROWS = 16
N_AUG = 3
CONV_WIDTH = 3
ROPE_BASE = 10000.0
NORM_EPS = 1e-6
GN_EPS = 1e-5
LOG2E = float(np.log2(np.e))
SEQ_TILE = 512
FOX_SUB = 256
FOX_QTILES = 4
RET_CHUNK = 256
RET_TILES = 2
MIX_ROWS = 256
FF_CHUNK = 1024
VMEM_LIMIT = 60 * 1024 * 1024


def _rms(x, w):
    return x * lax.rsqrt(jnp.mean(x * x, axis=-1, keepdims=True) + NORM_EPS) * w


def _split3(x):
    hi = x.astype(BF16).astype(F32)
    r = x - hi
    mid = r.astype(BF16).astype(F32)
    lo = (r - mid).astype(BF16).astype(F32)
    return hi, mid, lo


def _inproj_kernel(h_ref, nw_ref, wn_ref, wt_ref, fb_ref, cosn_ref, sina_ref, sinb_ref, cost_ref, sint_ref,
                   fk_ref, rq_ref, rv_ref, rg_ref, fqt_ref, fvt_ref, rkt_ref, cum_ref, carry_ref):
    tm = h_ref.shape[0]

    @pl.when(pl.program_id(1) == 0)
    def _():
        carry_ref[...] = jnp.zeros_like(carry_ref)

    u = _rms(h_ref[...], nw_ref[...]).astype(BF16)
    pn = jnp.dot(u, wn_ref[...], preferred_element_type=F32)
    pt = lax.dot_general(wt_ref[...], u, (((1,), (1,)), ((), ())),
                         preferred_element_type=F32)

    fk_ref[...] = pn[:, 0:WIDTH].astype(BF16)
    rv_ref[...] = pn[:, 2 * WIDTH:3 * WIDTH].astype(BF16)
    rg = pn[:, 3 * WIDTH:4 * WIDTH]
    rg_ref[...] = (rg * jax.nn.sigmoid(rg)).astype(BF16)

    cosn, sina, sinb = cosn_ref[...], sina_ref[...], sinb_ref[...]
    for g in range(N_PAIRS):
        xg = pn[:, WIDTH + g * LANES:WIDTH + (g + 1) * LANES]
        o = xg * cosn + pltpu.roll(xg, LANES - HALF, 1) * sina + pltpu.roll(xg, HALF, 1) * sinb
        rq_ref[:, g * LANES:(g + 1) * LANES] = o.astype(BF16)

    fqt_ref[...] = pt[0:WIDTH].astype(BF16)
    fvt_ref[...] = pt[WIDTH:2 * WIDTH].astype(BF16)
    cost, sint = cost_ref[...], sint_ref[...]
    for hd in range(N_HEADS):
        r0 = 2 * WIDTH + hd * HEAD_DIM
        t1, t2 = pt[r0:r0 + HALF], pt[r0 + HALF:r0 + HEAD_DIM]
        rkt_ref[hd * HEAD_DIM:hd * HEAD_DIM + HALF, :] = (t1 * cost - t2 * sint).astype(BF16)
        rkt_ref[hd * HEAD_DIM + HALF:(hd + 1) * HEAD_DIM, :] = (t1 * sint + t2 * cost).astype(BF16)

    logit = pt[3 * WIDTH:3 * WIDTH + N_HEADS] + fb_ref[...]
    logf = jnp.minimum(logit, 0.0) - jnp.log1p(jnp.exp(-jnp.abs(logit)))
    parts = jnp.concatenate(list(_split3(logf)) + [jnp.zeros_like(logf)], axis=0).astype(BF16)
    row = lax.broadcasted_iota(jnp.int32, (tm, tm), 0)
    col = lax.broadcasted_iota(jnp.int32, (tm, tm), 1)
    tri = jnp.where(row <= col, 1.0, 0.0).astype(BF16)
    cs = jnp.dot(parts, tri, preferred_element_type=F32)
    cum_ref[...] = cs[0:8] + cs[8:16] + cs[16:24] + carry_ref[...]
    carry_ref[...] = carry_ref[...] + jnp.sum(logf, axis=1, keepdims=True)


def _layer_spec(a, layer):
    return pl.BlockSpec((None,) + a.shape[1:], lambda b, s: (layer,) + (0,) * (a.ndim - 1),
                        pipeline_mode=pl.Buffered(1))


def _inproj(h, nw, wn, wt, fb, tabs, tm, layer):
    B, S, D = h.shape
    ns = S // tm
    cosn, sina, sinb, cost, sint = tabs
    tok = lambda w: pl.BlockSpec((None, tm, w), lambda b, s: (b, s, 0))
    full = lambda a: _layer_spec(a, layer)
    ttile = pl.BlockSpec((None, None, WIDTH, tm), lambda b, s: (b, s, 0, 0))
    nat = jax.ShapeDtypeStruct((B, S, WIDTH), BF16)
    ttl = jax.ShapeDtypeStruct((B, ns, WIDTH, tm), BF16)
    return pl.pallas_call(
        _inproj_kernel,
        grid=(B, ns),
        in_specs=[tok(D), full(nw), full(wn), full(wt), full(fb),
                  pl.BlockSpec((tm, LANES), lambda b, s: (s, 0)),
                  pl.BlockSpec((tm, LANES), lambda b, s: (s, 0)),
                  pl.BlockSpec((tm, LANES), lambda b, s: (s, 0)),
                  pl.BlockSpec((HALF, tm), lambda b, s: (0, s)),
                  pl.BlockSpec((HALF, tm), lambda b, s: (0, s))],
        out_specs=[tok(WIDTH), tok(WIDTH), tok(WIDTH), tok(WIDTH), ttile, ttile, ttile,
                   pl.BlockSpec((None, None, N_HEADS, tm), lambda b, s: (b, s, 0, 0))],
        out_shape=[nat, nat, nat, nat, ttl, ttl, ttl, jax.ShapeDtypeStruct((B, ns, N_HEADS, tm), F32)],
        scratch_shapes=[pltpu.VMEM((N_HEADS, 1), F32)],
        compiler_params=pltpu.CompilerParams(dimension_semantics=("parallel", "arbitrary"),
                                             vmem_limit_bytes=VMEM_LIMIT),
        name="inproj",
    )(h, nw, wn, wt, fb, cosn, sina, sinb, cost, sint)


def _fox_kernel(k_ref, qt_ref, vt_ref, cum_ref, o_ref, kaug_ref, vaug_ref, st_ref, m_ref, acc_ref):
    nk, _, tk = vt_ref.shape
    frow = lax.broadcasted_iota(jnp.int32, (LANES, tk), 0)
    klane = lax.broadcasted_iota(jnp.int32, (tk, LANES), 1)
    aug0 = (HEAD_DIM, 0)

    def build(j, carry):
        k0 = pl.multiple_of(j * tk, tk)
        kt = k_ref[pl.ds(k0, tk), :].astype(F32)
        vt = vt_ref[j].astype(F32)
        terms = _split3(cum_ref[j] * (-LOG2E))
        for hd in range(2):
            own = (klane < HEAD_DIM) if hd == 0 else (klane >= HEAD_DIM)
            own_t = (frow < HEAD_DIM) if hd == 0 else (frow >= HEAD_DIM)
            a = jnp.zeros((LANES, tk), F32)
            for t in range(N_AUG):
                a = jnp.where(frow == aug0[hd] + t, terms[t][hd:hd + 1], a)
            kaug_ref[hd, pl.ds(k0, tk), :] = (jnp.where(own, kt, 0.0) + a.T).astype(BF16)
            vaug_ref[hd, j] = jnp.where(own_t, vt, 1.0).astype(BF16)
        return carry

    lax.fori_loop(0, nk, build, 0)

    sub = FOX_SUB
    ks_per = tk // sub
    qs_per = FOX_QTILES * ks_per
    tq = FOX_QTILES * tk
    qrow = lax.broadcasted_iota(jnp.int32, (LANES, tk), 0)
    full_jobs = [(hd, qh) for qh in range(qs_per) for hd in range(2)]

    def diag_jobs(d):
        return [(hd, qh) for (hd, qh) in full_jobs if d * ks_per <= qh]

    def q_block(qi, carry):
        qa = []
        for t in range(FOX_QTILES):
            qt = qt_ref[qi * FOX_QTILES + t].astype(F32)
            qa.append([jnp.where(qrow < HEAD_DIM, qt, jnp.where(qrow < HEAD_DIM + N_AUG, 1.0, 0.0)).astype(BF16),
                       jnp.where(qrow >= HEAD_DIM, qt, jnp.where(qrow < N_AUG, 1.0, 0.0)).astype(BF16)])
        m_ref[...] = jnp.full(m_ref.shape, -jnp.inf, F32)
        acc_ref[...] = jnp.zeros(acc_ref.shape, F32)

        def issue_qk(slot, kj, job):
            hd, qh = job
            ks = kaug_ref[hd, pl.ds(pl.multiple_of(kj * tk, tk), tk), :]
            qsub = qa[qh // ks_per][hd][:, (qh % ks_per) * sub:(qh % ks_per + 1) * sub]
            st_ref[slot, full_jobs.index(job)] = jnp.dot(ks, qsub, preferred_element_type=F32)

        def consume(slot, kj, job, diag):
            hd, qh = job
            st = st_ref[slot, full_jobs.index(job)]
            if diag is not None and (diag + 1) * ks_per > qh:
                ahead = (lax.broadcasted_iota(jnp.int32, (tk, sub), 0)
                         - lax.broadcasted_iota(jnp.int32, (tk, sub), 1))
                st = jnp.where(ahead <= qh * sub - diag * tk, st, -jnp.inf)
            qs = slice(qh * sub, (qh + 1) * sub)
            m_prev = m_ref[hd, :, qs]
            m_new = jnp.maximum(m_prev, jnp.max(st, axis=0, keepdims=True))
            m_ref[hd, :, qs] = m_new
            alpha = jnp.exp2(m_prev - m_new)
            p = jnp.exp2(st - m_new).astype(BF16)
            acc_ref[hd, :, qs] = alpha * acc_ref[hd, :, qs] + jnp.dot(vaug_ref[hd, kj], p,
                                                                      preferred_element_type=F32)

        def step(slot, kj, jobs, diag, next_jobs):
            for n in range(max(len(jobs), len(next_jobs))):
                if n < len(next_jobs):
                    issue_qk(1 - slot, kj + 1, next_jobs[n])
                if n < len(jobs):
                    consume(slot, kj, jobs[n], diag)

        kbase = qi * FOX_QTILES
        for job in full_jobs:
            issue_qk(0, 0, job)

        def k_body(t, c):
            for r in range(FOX_QTILES):
                step(r % 2, t * FOX_QTILES + r, full_jobs, None, full_jobs)
            return c

        lax.fori_loop(0, qi, k_body, 0)
        for d in range(FOX_QTILES):
            jobs = diag_jobs(d)
            step(d % 2, kbase + d, jobs, d, diag_jobs(d + 1) if d + 1 < FOX_QTILES else [])
        a0, a1 = acc_ref[0], acc_ref[1]
        orow = lax.broadcasted_iota(jnp.int32, a0.shape, 0)
        ot = jnp.where(orow < HEAD_DIM, a0 / a0[HEAD_DIM:HEAD_DIM + 1], a1 / a1[0:1])
        o_ref[pl.ds(pl.multiple_of(qi * tq, tq), tq), :] = ot.T.astype(o_ref.dtype)
        return carry

    lax.fori_loop(0, nk // FOX_QTILES, q_block, 0)


def _fox(fk, fqt, fvt, cum):
    B, S, _ = fk.shape
    _, nk, _, tk = fqt.shape
    cum5 = cum.reshape(B, nk, N_PAIRS, 2, tk)
    assert FOX_QTILES % 2 == 0 and nk % FOX_QTILES == 0 and tk % FOX_SUB == 0
    tq = FOX_QTILES * tk
    n_jobs = tq // FOX_SUB
    nat = pl.BlockSpec((None, S, LANES), lambda b, p: (b, 0, p))
    ttile = pl.BlockSpec((None, nk, LANES, tk), lambda b, p: (b, 0, p, 0))
    return pl.pallas_call(
        _fox_kernel,
        grid=(B, N_PAIRS),
        in_specs=[nat, ttile, ttile,
                  pl.BlockSpec((None, nk, None, 2, tk), lambda b, p: (b, 0, p, 0, 0))],
        out_specs=nat,
        out_shape=jax.ShapeDtypeStruct((B, S, WIDTH), BF16),
        scratch_shapes=[pltpu.VMEM((2, S, LANES), BF16), pltpu.VMEM((2, nk, LANES, tk), BF16),
                        pltpu.VMEM((2, 2 * n_jobs, tk, FOX_SUB), F32),
                        pltpu.VMEM((2, 1, tq), F32), pltpu.VMEM((2, LANES, tq), F32)],
        compiler_params=pltpu.CompilerParams(dimension_semantics=("parallel", "parallel"),
                                             vmem_limit_bytes=VMEM_LIMIT),
        name="fox_attention",
    )(fk, fqt, fvt, cum5)


def _ret_kernel(q_ref, kt_ref, v_ref, g_ref, dm_ref, qd_ref, kd_ref, cd_ref, bd_ref, gn_ref, o_ref, st_ref):
    nk, _, tk = kt_ref.shape
    C = dm_ref.shape[-1]
    lane = lax.broadcasted_iota(jnp.int32, (C, LANES), 1)
    st_ref[...] = jnp.zeros_like(st_ref)
    gn = gn_ref[...]

    def group_mean(y):
        return jnp.dot(y.astype(BF16), gn, preferred_element_type=F32)

    def trip(j, carry):
        chunks = []
        for t in range(RET_TILES):
            kt_tile = kt_ref[j * RET_TILES + t]
            for sc in range(tk // C):
                r0 = pl.multiple_of((j * RET_TILES + t) * tk + sc * C, C)
                chunks.append((r0, q_ref[pl.ds(r0, C), :], v_ref[pl.ds(r0, C), :],
                               kt_tile[:, sc * C:(sc + 1) * C]))
        scores, kvs = [], []
        for r0, q2, v2, kt in chunks:
            q2f = q2.astype(F32)
            sc_pair = []
            for hd in range(2):
                keep = (lane < HEAD_DIM) if hd == 0 else (lane >= HEAD_DIM)
                qm = jnp.where(keep, q2f, 0.0).astype(BF16)
                sc_pair.append((jnp.dot(qm, kt, preferred_element_type=F32) * dm_ref[hd]).astype(BF16))
            scores.append(sc_pair)
            kdec = (kt.astype(F32) * kd_ref[...]).astype(BF16)
            kvs.append(jnp.dot(kdec, v2, preferred_element_type=F32) * bd_ref[...])
        st = st_ref[...]
        crosses = []
        for (r0, q2, v2, kt), kv in zip(chunks, kvs):
            crosses.append(jnp.dot(q2, st.astype(BF16), preferred_element_type=F32) * qd_ref[...])
            st = st * cd_ref[...] + kv
        st_ref[...] = st
        ys = []
        for (r0, q2, v2, kt), sc_pair, cross in zip(chunks, scores, crosses):
            inner = [jnp.dot(s, v2, preferred_element_type=F32) for s in sc_pair]
            ys.append(jnp.where(lane < HEAD_DIM, inner[0], inner[1]) + cross)
        ds = [y - group_mean(y) for y in ys]
        vs = [group_mean(d * d) for d in ds]
        for (r0, q2, v2, kt), d, var in zip(chunks, ds, vs):
            out = d * lax.rsqrt(var + GN_EPS) * g_ref[pl.ds(r0, C), :].astype(F32)
            o_ref[pl.ds(r0, C), :] = out.astype(o_ref.dtype)
        return carry

    lax.fori_loop(0, nk // RET_TILES, trip, 0)


def _ret(rq, rkt, rv, rg, tabs):
    B, S, _ = rq.shape
    _, nk, _, tk = rkt.shape
    dm, qd, kd, cd, bd, gn = tabs
    C = dm.shape[-1]
    assert nk % RET_TILES == 0 and tk % C == 0
    nat = pl.BlockSpec((None, S, LANES), lambda b, p: (b, 0, p))
    return pl.pallas_call(
        _ret_kernel,
        grid=(B, N_PAIRS),
        in_specs=[nat,
                  pl.BlockSpec((None, nk, LANES, tk), lambda b, p: (b, 0, p, 0)),
                  nat, nat,
                  pl.BlockSpec((None, 2, C, C), lambda b, p: (p, 0, 0, 0)),
                  pl.BlockSpec((None, C, LANES), lambda b, p: (p, 0, 0)),
                  pl.BlockSpec((None, LANES, C), lambda b, p: (p, 0, 0)),
                  pl.BlockSpec((None, LANES, LANES), lambda b, p: (p, 0, 0)),
                  pl.BlockSpec((LANES, LANES), lambda b, p: (0, 0)),
                  pl.BlockSpec((LANES, LANES), lambda b, p: (0, 0))],
        out_specs=nat,
        out_shape=jax.ShapeDtypeStruct((B, S, WIDTH), BF16),
        scratch_shapes=[pltpu.VMEM((LANES, LANES), F32)],
        compiler_params=pltpu.CompilerParams(dimension_semantics=("parallel", "parallel"),
                                             vmem_limit_bytes=VMEM_LIMIT),
        name="retention",
    )(rq, rkt, rv, rg, dm, qd, kd, cd, bd, gn)


def _mixer_kernel(h_ref, fox_ref, ret_ref, p_ref, wo_ref, fnw_ref, wa_ref, wg_ref, cw_ref, cb_ref, wd_ref,
                  pnw_ref, wpg_ref, wpp_ref, onw_ref, o_ref, abuf_ref, tail_ref, *, final_norm):
    tm = h_ref.shape[0]
    d_ff = wa_ref.shape[1]
    halo = tail_ref.shape[0]

    @pl.when(pl.program_id(1) == 0)
    def _():
        tail_ref[...] = jnp.zeros_like(tail_ref)

    strands = [slice(r, r + MIX_ROWS) for r in range(0, tm, MIX_ROWS)]
    h1s, us = [], []
    for rs in strands:
        h1 = (h_ref[rs, :] + jnp.dot(fox_ref[rs, :], wo_ref[0:WIDTH], preferred_element_type=F32)
              + jnp.dot(ret_ref[rs, :], wo_ref[WIDTH:2 * WIDTH], preferred_element_type=F32))
        h1s.append(h1)
        us.append(_rms(h1, fnw_ref[...]).astype(BF16))
    downs = [None] * len(strands)
    for c0 in range(0, d_ff, FF_CHUNK):
        c1 = min(c0 + FF_CHUNK, d_ff)
        w = c1 - c0
        abuf = abuf_ref.at[(c0 // FF_CHUNK) % 2]
        abuf[0:halo, 0:w] = tail_ref[:, c0:c1]
        gs = []
        for i, rs in enumerate(strands):
            a = jnp.dot(us[i], wa_ref[:, c0:c1], preferred_element_type=F32)
            gs.append(jnp.dot(us[i], wg_ref[:, c0:c1], preferred_element_type=F32))
            abuf[halo + rs.start:halo + rs.stop, 0:w] = a
            if rs.stop == tm:
                tail_ref[:, c0:c1] = a[MIX_ROWS - halo:MIX_ROWS]
        for i, rs in enumerate(strands):
            y = cb_ref[:, c0:c1]
            for t in range(CONV_WIDTH):
                off = halo - (CONV_WIDTH - 1) + t + rs.start
                y = y + abuf[off:off + MIX_ROWS, 0:w] * cw_ref[t:t + 1, c0:c1]
            act = (0.5 * y * (1.0 + lax.erf(y * np.float32(1.0 / np.sqrt(2.0)))) * gs[i]).astype(BF16)
            part = jnp.dot(act, wd_ref[c0:c1, :], preferred_element_type=F32)
            downs[i] = part if downs[i] is None else downs[i] + part
    for i, rs in enumerate(strands):
        h2 = h1s[i] + downs[i]
        n = _rms(h2, pnw_ref[...]).astype(BF16)
        gate = jax.nn.sigmoid(jnp.dot(n, wpg_ref[...], preferred_element_type=F32))
        pe = jnp.dot(p_ref[rs, :].astype(BF16), wpp_ref[...], preferred_element_type=F32)
        h3 = h2 + gate * pe
        if final_norm:
            h3 = _rms(h3, onw_ref[...])
        o_ref[rs, :] = h3


def _mixer(h, fox, ret, p, params, d_ff, tm, layer, final_norm):
    B, S, D = h.shape
    wo, fnw, wup, cw, cb, wd, pnw, wpg, wpp, onw = params
    tok = lambda w: pl.BlockSpec((None, tm, w), lambda b, s: (b, s, 0))
    res = lambda a: _layer_spec(a, layer)
    up_half = lambda j: pl.BlockSpec((None, D, d_ff), lambda b, s: (layer, 0, j), pipeline_mode=pl.Buffered(1))
    return pl.pallas_call(
        functools.partial(_mixer_kernel, final_norm=final_norm),
        grid=(B, S // tm),
        in_specs=[tok(D), tok(WIDTH), tok(WIDTH),
                  pl.BlockSpec((None, None, tm, p.shape[-1]), lambda b, s: (layer, b, s, 0)),
                  res(wo), res(fnw), up_half(0), up_half(1), res(cw), res(cb), res(wd), res(pnw), res(wpg),
                  res(wpp), _layer_spec(onw, 0)],
        out_specs=tok(D),
        out_shape=jax.ShapeDtypeStruct((B, S, D), F32),
        scratch_shapes=[pltpu.VMEM((2, tm + 8, FF_CHUNK), F32), pltpu.VMEM((8, d_ff), F32)],
        compiler_params=pltpu.CompilerParams(dimension_semantics=("parallel", "arbitrary"),
                                             vmem_limit_bytes=VMEM_LIMIT),
        name="channel_mixer",
    )(h, fox, ret, p, wo, fnw, wup, wup, cw, cb, wd, pnw, wpg, wpp, onw)


def _rotary_tables(seq):
    inv_freq = ROPE_BASE ** (-jnp.arange(0, HEAD_DIM, 2, dtype=F32) / HEAD_DIM)
    ang = jnp.arange(seq, dtype=F32)[:, None] * inv_freq[None, :]
    cos, sin = jnp.cos(ang), jnp.sin(ang)
    first = (jnp.arange(LANES) % HEAD_DIM) < HALF
    cosn = jnp.tile(cos, (1, LANES // HALF))
    sinn = jnp.tile(sin, (1, LANES // HALF))
    sina = jnp.where(first[None, :], -sinn, 0.0)
    sinb = jnp.where(first[None, :], 0.0, sinn)
    return cosn, sina, sinb, cos.T, sin.T


def _retention_tables(C):
    log_gamma = jnp.log1p(-jnp.exp2(-5.0 - jnp.arange(N_HEADS, dtype=F32)))
    idx = jnp.arange(C, dtype=F32)
    rel = idx[:, None] - idx[None, :]
    lg = log_gamma[:, None, None]
    dm = jnp.where(rel >= 0, jnp.exp(lg * jnp.maximum(rel, 0.0)), 0.0).reshape(N_PAIRS, 2, C, C)
    q_decay = jnp.exp(log_gamma[:, None] * (idx + 1.0))
    k_decay = jnp.exp(log_gamma[:, None] * (C - 1.0 - idx))
    chunk_decay = jnp.exp(log_gamma * C)
    qd = jnp.repeat(q_decay.reshape(N_PAIRS, 2, C), HEAD_DIM, axis=1).transpose(0, 2, 1)
    kd = jnp.repeat(k_decay.reshape(N_PAIRS, 2, C), HEAD_DIM, axis=1)
    blk = jnp.arange(LANES) // HEAD_DIM
    bd = (blk[:, None] == blk[None, :]).astype(F32)
    cd = jnp.repeat(chunk_decay.reshape(N_PAIRS, 2), HEAD_DIM, axis=1)[:, :, None] * bd[None]
    gn = (bd / HEAD_DIM).astype(BF16)
    return dm, qd, kd, cd, bd, gn


def kernel(x, p, attn_norm_w, w_in, forget_bias, w_out, ffn_norm_w, w_up, conv_w, conv_b, w_down, ple_norm_w,
           w_ple_gate, w_ple_proj, final_norm_w):
    B, S, D = x.shape
    depth = w_in.shape[0]
    d_ff = w_down.shape[1]
    tm = min(SEQ_TILE, S)
    assert S % tm == 0 and tm % RET_CHUNK == 0 and D == 2 * WIDTH
    scale = HEAD_DIM ** -0.5
    rot = _rotary_tables(S)
    rtab = _retention_tables(RET_CHUNK)

    o = [0, WIDTH, 2 * WIDTH, 3 * WIDTH, 3 * WIDTH + N_HEADS]
    o += [o[-1] + WIDTH, o[-1] + 2 * WIDTH, o[-1] + 3 * WIDTH, o[-1] + 4 * WIDTH]
    rows = lambda v: v.reshape(depth, 1, -1)
    col = lambda k: w_in[:, :, o[k]:o[k + 1]]
    wn = jnp.concatenate([col(1), col(4), col(6), col(7)], axis=2).astype(BF16)
    wt = jnp.concatenate([col(0) * (scale * LOG2E), col(2), col(5) * scale, col(3),
                          jnp.zeros((depth, D, BF16_ROWS - N_HEADS), F32)], axis=2).transpose(0, 2, 1).astype(BF16)
    mixer_params = (w_out.astype(BF16), rows(ffn_norm_w), w_up.astype(BF16), conv_w, rows(conv_b),
                    w_down.astype(BF16), rows(ple_norm_w), w_ple_gate.astype(BF16), w_ple_proj.astype(BF16),
                    final_norm_w.reshape(1, 1, -1))
    attn_nw, fbias = rows(attn_norm_w), forget_bias.reshape(depth, N_HEADS, 1)
    h = x
    for i in range(depth):
        fk, rq, rv, rg, fqt, fvt, rkt, cum = _inproj(h, attn_nw, wn, wt, fbias, rot, tm, i)
        fox = _fox(fk, fqt, fvt, cum)
        ret = _ret(rq, rkt, rv, rg, rtab)
        h = _mixer(h, fox, ret, p, mixer_params, d_ff, tm, i, final_norm=(i == depth - 1))
    return h
```

```python
import functools

import numpy as np
import jax
import jax.numpy as jnp
from jax import lax
from jax.experimental import pallas as pl
from jax.experimental.pallas import tpu as pltpu

F32 = jnp.float32
BF16 = jnp.bfloat16

HEAD_DIM = 64
HALF = HEAD_DIM // 2
N_HEADS = 8
WIDTH = N_HEADS * HEAD_DIM
LANES = 128
N_PAIRS = WIDTH // LANES
BF16_ROWS = 16
N_AUG = 3
CONV_WIDTH = 3
ROPE_BASE = 10000.0
NORM_EPS = 1e-6
GN_EPS = 1e-5
LOG2E = float(np.log2(np.e))
SEQ_TILE = 512
FOX_SUB = 256
FOX_QTILES = 4
RET_CHUNK = 256
RET_TILES = 2
MIX_ROWS = 256
FF_CHUNK = 1024
VMEM_LIMIT = 60 * 1024 * 1024


def _rms(x, w):
    return x * lax.rsqrt(jnp.mean(x * x, axis=-1, keepdims=True) + NORM_EPS) * w


def _split3(x):
    hi = x.astype(BF16).astype(F32)
    r = x - hi
    mid = r.astype(BF16).astype(F32)
    lo = (r - mid).astype(BF16).astype(F32)
    return hi, mid, lo


def _inproj_kernel(h_ref, nw_ref, wn_ref, wt_ref, fb_ref, cosn_ref, sina_ref, sinb_ref, cost_ref, sint_ref,
                   fk_ref, rq_ref, rv_ref, rg_ref, fqt_ref, fvt_ref, rkt_ref, cum_ref, carry_ref):
    tm = h_ref.shape[0]

    @pl.when(pl.program_id(1) == 0)
    def _():
        carry_ref[...] = jnp.zeros_like(carry_ref)

    logits = []
    for r in range(0, tm, MIX_ROWS):
        rs = slice(r, r + MIX_ROWS)
        u = _rms(h_ref[rs, :], nw_ref[...]).astype(BF16)
        pn = jnp.dot(u, wn_ref[...], preferred_element_type=F32)
        pt = lax.dot_general(wt_ref[...], u, (((1,), (1,)), ((), ())),
                             preferred_element_type=F32)

        fk_ref[rs, :] = pn[:, 0:WIDTH].astype(BF16)
        rv_ref[rs, :] = pn[:, 2 * WIDTH:3 * WIDTH].astype(BF16)
        rg = pn[:, 3 * WIDTH:4 * WIDTH]
        rg_ref[rs, :] = (rg * jax.nn.sigmoid(rg)).astype(BF16)

        cosn, sina, sinb = cosn_ref[rs, :], sina_ref[rs, :], sinb_ref[rs, :]
        for g in range(N_PAIRS):
            xg = pn[:, WIDTH + g * LANES:WIDTH + (g + 1) * LANES]
            o = xg * cosn + pltpu.roll(xg, LANES - HALF, 1) * sina + pltpu.roll(xg, HALF, 1) * sinb
            rq_ref[rs, g * LANES:(g + 1) * LANES] = o.astype(BF16)

        fqt_ref[:, rs] = pt[0:WIDTH].astype(BF16)
        fvt_ref[:, rs] = pt[WIDTH:2 * WIDTH].astype(BF16)
        cost, sint = cost_ref[:, rs], sint_ref[:, rs]
        for hd in range(N_HEADS):
            r0 = 2 * WIDTH + hd * HEAD_DIM
            t1, t2 = pt[r0:r0 + HALF], pt[r0 + HALF:r0 + HEAD_DIM]
            rkt_ref[hd * HEAD_DIM:hd * HEAD_DIM + HALF, rs] = (t1 * cost - t2 * sint).astype(BF16)
            rkt_ref[hd * HEAD_DIM + HALF:(hd + 1) * HEAD_DIM, rs] = (t1 * sint + t2 * cost).astype(BF16)
        logits.append(pt[3 * WIDTH:3 * WIDTH + N_HEADS])

    logit = jnp.concatenate(logits, axis=1) + fb_ref[...]
    logf = jnp.minimum(logit, 0.0) - jnp.log1p(jnp.exp(-jnp.abs(logit)))
    parts = jnp.concatenate(list(_split3(logf)) + [jnp.zeros_like(logf)], axis=0).astype(BF16)
    row = lax.broadcasted_iota(jnp.int32, (tm, tm), 0)
    col = lax.broadcasted_iota(jnp.int32, (tm, tm), 1)
    tri = jnp.where(row <= col, 1.0, 0.0).astype(BF16)
    cs = jnp.dot(parts, tri, preferred_element_type=F32)
    cum_ref[...] = cs[0:8] + cs[8:16] + cs[16:24] + carry_ref[...]
    carry_ref[...] = carry_ref[...] + jnp.sum(logf, axis=1, keepdims=True)


def _layer_spec(a, layer):
    return pl.BlockSpec((None,) + a.shape[1:], lambda b, s: (layer,) + (0,) * (a.ndim - 1),
                        pipeline_mode=pl.Buffered(1))


def _inproj(h, nw, wn, wt, fb, tabs, tm, layer):
    B, S, D = h.shape
    ns = S // tm
    cosn, sina, sinb, cost, sint = tabs
    tok = lambda w: pl.BlockSpec((None, tm, w), lambda b, s: (b, s, 0))
    full = lambda a: _layer_spec(a, layer)
    ttile = pl.BlockSpec((None, None, WIDTH, tm), lambda b, s: (b, s, 0, 0))
    nat = jax.ShapeDtypeStruct((B, S, WIDTH), BF16)
    ttl = jax.ShapeDtypeStruct((B, ns, WIDTH, tm), BF16)
    return pl.pallas_call(
        _inproj_kernel,
        grid=(B, ns),
        in_specs=[tok(D), full(nw), full(wn), full(wt), full(fb),
                  pl.BlockSpec((tm, LANES), lambda b, s: (s, 0)),
                  pl.BlockSpec((tm, LANES), lambda b, s: (s, 0)),
                  pl.BlockSpec((tm, LANES), lambda b, s: (s, 0)),
                  pl.BlockSpec((HALF, tm), lambda b, s: (0, s)),
                  pl.BlockSpec((HALF, tm), lambda b, s: (0, s))],
        out_specs=[tok(WIDTH), tok(WIDTH), tok(WIDTH), tok(WIDTH), ttile, ttile, ttile,
                   pl.BlockSpec((None, None, N_HEADS, tm), lambda b, s: (b, s, 0, 0))],
        out_shape=[nat, nat, nat, nat, ttl, ttl, ttl, jax.ShapeDtypeStruct((B, ns, N_HEADS, tm), F32)],
        scratch_shapes=[pltpu.VMEM((N_HEADS, 1), F32)],
        compiler_params=pltpu.CompilerParams(dimension_semantics=("parallel", "arbitrary"),
                                             vmem_limit_bytes=VMEM_LIMIT),
        name="inproj",
    )(h, nw, wn, wt, fb, cosn, sina, sinb, cost, sint)


def _fox_kernel(k_ref, qt_ref, vt_ref, cum_ref, o_ref, kaug_ref, vaug_ref, st_ref, m_ref, acc_ref):
    nk, _, tk = vt_ref.shape
    frow = lax.broadcasted_iota(jnp.int32, (LANES, tk), 0)
    klane = lax.broadcasted_iota(jnp.int32, (tk, LANES), 1)
    aug0 = (HEAD_DIM, 0)

    def build(j, carry):
        k0 = pl.multiple_of(j * tk, tk)
        kt = k_ref[pl.ds(k0, tk), :].astype(F32)
        vt = vt_ref[j].astype(F32)
        terms = _split3(cum_ref[j] * (-LOG2E))
        for hd in range(2):
            own = (klane < HEAD_DIM) if hd == 0 else (klane >= HEAD_DIM)
            own_t = (frow < HEAD_DIM) if hd == 0 else (frow >= HEAD_DIM)
            a = jnp.zeros((LANES, tk), F32)
            for t in range(N_AUG):
                a = jnp.where(frow == aug0[hd] + t, terms[t][hd:hd + 1], a)
            kaug_ref[hd, pl.ds(k0, tk), :] = (jnp.where(own, kt, 0.0) + a.T).astype(BF16)
            vaug_ref[hd, j] = jnp.where(own_t, vt, 1.0).astype(BF16)
        return carry

    lax.fori_loop(0, nk, build, 0)

    sub = FOX_SUB
    ks_per = tk // sub
    qs_per = FOX_QTILES * ks_per
    tq = FOX_QTILES * tk
    qrow = lax.broadcasted_iota(jnp.int32, (LANES, tk), 0)
    full_jobs = [(hd, qh) for qh in range(qs_per) for hd in range(2)]

    def diag_jobs(d):
        return [(hd, qh) for (hd, qh) in full_jobs if d * ks_per <= qh]

    def q_block(qi, carry):
        qa = []
        for t in range(FOX_QTILES):
            qt = qt_ref[qi * FOX_QTILES + t].astype(F32)
            qa.append([jnp.where(qrow < HEAD_DIM, qt, jnp.where(qrow < HEAD_DIM + N_AUG, 1.0, 0.0)).astype(BF16),
                       jnp.where(qrow >= HEAD_DIM, qt, jnp.where(qrow < N_AUG, 1.0, 0.0)).astype(BF16)])
        m_ref[...] = jnp.full(m_ref.shape, -jnp.inf, F32)
        acc_ref[...] = jnp.zeros(acc_ref.shape, F32)

        def live_keys(qh, diag):
            return tk if diag is None else min(tk, (qh - diag * ks_per + 1) * sub)

        def issue_qk(slot, kj, job, diag=None):
            hd, qh = job
            nkeys = live_keys(qh, diag)
            ks = kaug_ref[hd, pl.ds(pl.multiple_of(kj * tk, tk), nkeys), :]
            qsub = qa[qh // ks_per][hd][:, (qh % ks_per) * sub:(qh % ks_per + 1) * sub]
            st_ref[slot, full_jobs.index(job), 0:nkeys, :] = jnp.dot(ks, qsub,
                                                                     preferred_element_type=F32)

        def consume(slot, kj, job, diag):
            hd, qh = job
            nkeys = live_keys(qh, diag)
            st = st_ref[slot, full_jobs.index(job), 0:nkeys, :]
            if diag is not None and (diag + 1) * ks_per > qh:
                ahead = (lax.broadcasted_iota(jnp.int32, (nkeys, sub), 0)
                         - lax.broadcasted_iota(jnp.int32, (nkeys, sub), 1))
                st = jnp.where(ahead <= qh * sub - diag * tk, st, -jnp.inf)
            qs = slice(qh * sub, (qh + 1) * sub)
            m_prev = m_ref[hd, :, qs]
            m_new = jnp.maximum(m_prev, jnp.max(st, axis=0, keepdims=True))
            m_ref[hd, :, qs] = m_new
            alpha = jnp.exp2(m_prev - m_new)
            p = jnp.exp2(st - m_new).astype(BF16)
            acc_ref[hd, :, qs] = alpha * acc_ref[hd, :, qs] + jnp.dot(vaug_ref[hd, kj, :, 0:nkeys], p,
                                                                      preferred_element_type=F32)

        def step(slot, kj, jobs, diag, next_jobs, next_diag=None):
            for n in range(max(len(jobs), len(next_jobs))):
                if n < len(next_jobs):
                    issue_qk(1 - slot, kj + 1, next_jobs[n], next_diag)
                if n < len(jobs):
                    consume(slot, kj, jobs[n], diag)

        kbase = qi * FOX_QTILES
        for job in full_jobs:
            issue_qk(0, 0, job)

        def k_body(t, c):
            for r in range(FOX_QTILES):
                step(r % 2, t * FOX_QTILES + r, full_jobs, None, full_jobs)
            return c

        lax.fori_loop(0, qi, k_body, 0)
        for d in range(FOX_QTILES):
            jobs = diag_jobs(d)
            step(d % 2, kbase + d, jobs, d, diag_jobs(d + 1) if d + 1 < FOX_QTILES else [], d + 1)
        a0, a1 = acc_ref[0], acc_ref[1]
        orow = lax.broadcasted_iota(jnp.int32, a0.shape, 0)
        ot = jnp.where(orow < HEAD_DIM, a0 / a0[HEAD_DIM:HEAD_DIM + 1], a1 / a1[0:1])
        o_ref[pl.ds(pl.multiple_of(qi * tq, tq), tq), :] = ot.T.astype(o_ref.dtype)
        return carry

    lax.fori_loop(0, nk // FOX_QTILES, q_block, 0)


def _fox(fk, fqt, fvt, cum):
    B, S, _ = fk.shape
    _, nk, _, tk = fqt.shape
    cum5 = cum.reshape(B, nk, N_PAIRS, 2, tk)
    assert FOX_QTILES % 2 == 0 and nk % FOX_QTILES == 0 and tk % FOX_SUB == 0
    tq = FOX_QTILES * tk
    n_jobs = tq // FOX_SUB
    nat = pl.BlockSpec((None, S, LANES), lambda b, p: (b, 0, p))
    ttile = pl.BlockSpec((None, nk, LANES, tk), lambda b, p: (b, 0, p, 0))
    return pl.pallas_call(
        _fox_kernel,
        grid=(B, N_PAIRS),
        in_specs=[nat, ttile, ttile,
                  pl.BlockSpec((None, nk, None, 2, tk), lambda b, p: (b, 0, p, 0, 0))],
        out_specs=nat,
        out_shape=jax.ShapeDtypeStruct((B, S, WIDTH), BF16),
        scratch_shapes=[pltpu.VMEM((2, S, LANES), BF16), pltpu.VMEM((2, nk, LANES, tk), BF16),
                        pltpu.VMEM((2, 2 * n_jobs, tk, FOX_SUB), F32),
                        pltpu.VMEM((2, 1, tq), F32), pltpu.VMEM((2, LANES, tq), F32)],
        compiler_params=pltpu.CompilerParams(dimension_semantics=("parallel", "parallel"),
                                             vmem_limit_bytes=VMEM_LIMIT),
        name="fox_attention",
    )(fk, fqt, fvt, cum5)


def _ret_kernel(q_ref, kt_ref, v_ref, g_ref, dm_ref, qd_ref, kd_ref, cd_ref, bd_ref, gn_ref, o_ref, st_ref):
    nk, _, tk = kt_ref.shape
    C = dm_ref.shape[-1]
    lane = lax.broadcasted_iota(jnp.int32, (C, LANES), 1)
    st_ref[...] = jnp.zeros_like(st_ref)
    gn = gn_ref[...]

    def group_mean(y):
        y_hi = y.astype(BF16)
        y_lo = (y - y_hi.astype(F32)).astype(BF16)
        return jnp.dot(y_hi, gn, preferred_element_type=F32) + jnp.dot(y_lo, gn, preferred_element_type=F32)

    def trip(j, carry):
        chunks = []
        for t in range(RET_TILES):
            kt_tile = kt_ref[j * RET_TILES + t]
            for sc in range(tk // C):
                r0 = pl.multiple_of((j * RET_TILES + t) * tk + sc * C, C)
                chunks.append((r0, q_ref[pl.ds(r0, C), :], v_ref[pl.ds(r0, C), :],
                               kt_tile[:, sc * C:(sc + 1) * C]))
        scores, kvs = [], []
        for r0, q2, v2, kt in chunks:
            q2f = q2.astype(F32)
            sc_pair = []
            for hd in range(2):
                keep = (lane < HEAD_DIM) if hd == 0 else (lane >= HEAD_DIM)
                qm = jnp.where(keep, q2f, 0.0).astype(BF16)
                sc_pair.append((jnp.dot(qm, kt, preferred_element_type=F32) * dm_ref[hd]).astype(BF16))
            scores.append(sc_pair)
            kdec = (kt.astype(F32) * kd_ref[...]).astype(BF16)
            kvs.append(jnp.dot(kdec, v2, preferred_element_type=F32) * bd_ref[...])
        st = st_ref[...]
        crosses = []
        for (r0, q2, v2, kt), kv in zip(chunks, kvs):
            crosses.append(jnp.dot(q2, st.astype(BF16), preferred_element_type=F32) * qd_ref[...])
            st = st * cd_ref[...] + kv
        st_ref[...] = st
        ys = []
        for (r0, q2, v2, kt), sc_pair, cross in zip(chunks, scores, crosses):
            inner = [jnp.dot(s, v2, preferred_element_type=F32) for s in sc_pair]
            ys.append(jnp.where(lane < HEAD_DIM, inner[0], inner[1]) + cross)
        ds = [y - group_mean(y) for y in ys]
        vs = [group_mean(d * d) for d in ds]
        for (r0, q2, v2, kt), d, var in zip(chunks, ds, vs):
            out = d * lax.rsqrt(var + GN_EPS) * g_ref[pl.ds(r0, C), :].astype(F32)
            o_ref[pl.ds(r0, C), :] = out.astype(o_ref.dtype)
        return carry

    lax.fori_loop(0, nk // RET_TILES, trip, 0)


def _ret(rq, rkt, rv, rg, tabs):
    B, S, _ = rq.shape
    _, nk, _, tk = rkt.shape
    dm, qd, kd, cd, bd, gn = tabs
    C = dm.shape[-1]
    assert nk % RET_TILES == 0 and tk % C == 0
    nat = pl.BlockSpec((None, S, LANES), lambda b, p: (b, 0, p))
    return pl.pallas_call(
        _ret_kernel,
        grid=(B, N_PAIRS),
        in_specs=[nat,
                  pl.BlockSpec((None, nk, LANES, tk), lambda b, p: (b, 0, p, 0)),
                  nat, nat,
                  pl.BlockSpec((None, 2, C, C), lambda b, p: (p, 0, 0, 0)),
                  pl.BlockSpec((None, C, LANES), lambda b, p: (p, 0, 0)),
                  pl.BlockSpec((None, LANES, C), lambda b, p: (p, 0, 0)),
                  pl.BlockSpec((None, LANES, LANES), lambda b, p: (p, 0, 0)),
                  pl.BlockSpec((LANES, LANES), lambda b, p: (0, 0)),
                  pl.BlockSpec((LANES, LANES), lambda b, p: (0, 0))],
        out_specs=nat,
        out_shape=jax.ShapeDtypeStruct((B, S, WIDTH), BF16),
        scratch_shapes=[pltpu.VMEM((LANES, LANES), F32)],
        compiler_params=pltpu.CompilerParams(dimension_semantics=("parallel", "parallel"),
                                             vmem_limit_bytes=VMEM_LIMIT),
        name="retention",
    )(rq, rkt, rv, rg, dm, qd, kd, cd, bd, gn)


def _mixer_kernel(h_ref, fox_ref, ret_ref, p_ref, wo_ref, fnw_ref, wa_ref, wg_ref, cw_ref, cb_ref, wd_ref,
                  pnw_ref, wpg_ref, wpp_ref, onw_ref, o_ref, abuf_ref, tail_ref, *, final_norm):
    tm = h_ref.shape[0]
    d_ff = wa_ref.shape[1]
    halo = tail_ref.shape[0]

    @pl.when(pl.program_id(1) == 0)
    def _():
        tail_ref[...] = jnp.zeros_like(tail_ref)

    strands = [slice(r, r + MIX_ROWS) for r in range(0, tm, MIX_ROWS)]
    h1s, us = [], []
    for rs in strands:
        h1 = (h_ref[rs, :] + jnp.dot(fox_ref[rs, :], wo_ref[0:WIDTH], preferred_element_type=F32)
              + jnp.dot(ret_ref[rs, :], wo_ref[WIDTH:2 * WIDTH], preferred_element_type=F32))
        h1s.append(h1)
        us.append(_rms(h1, fnw_ref[...]).astype(BF16))
    downs = [None] * len(strands)
    for c0 in range(0, d_ff, FF_CHUNK):
        c1 = min(c0 + FF_CHUNK, d_ff)
        w = c1 - c0
        abuf = abuf_ref.at[(c0 // FF_CHUNK) % 2]
        abuf[0:halo, 0:w] = tail_ref[:, c0:c1]
        gs = []
        for i, rs in enumerate(strands):
            a = jnp.dot(us[i], wa_ref[:, c0:c1], preferred_element_type=F32)
            gs.append(jnp.dot(us[i], wg_ref[:, c0:c1], preferred_element_type=F32))
            abuf[halo + rs.start:halo + rs.stop, 0:w] = a
            if rs.stop == tm:
                tail_ref[:, c0:c1] = a[MIX_ROWS - halo:MIX_ROWS]
        for i, rs in enumerate(strands):
            y = cb_ref[:, c0:c1]
            for t in range(CONV_WIDTH):
                off = halo - (CONV_WIDTH - 1) + t + rs.start
                y = y + abuf[off:off + MIX_ROWS, 0:w] * cw_ref[t:t + 1, c0:c1]
            act = (0.5 * y * (1.0 + lax.erf(y * np.float32(1.0 / np.sqrt(2.0)))) * gs[i]).astype(BF16)
            part = jnp.dot(act, wd_ref[c0:c1, :], preferred_element_type=F32)
            downs[i] = part if downs[i] is None else downs[i] + part
    for i, rs in enumerate(strands):
        h2 = h1s[i] + downs[i]
        n = _rms(h2, pnw_ref[...]).astype(BF16)
        gate = jax.nn.sigmoid(jnp.dot(n, wpg_ref[...], preferred_element_type=F32))
        pe = jnp.dot(p_ref[rs, :].astype(BF16), wpp_ref[...], preferred_element_type=F32)
        h3 = h2 + gate * pe
        if final_norm:
            h3 = _rms(h3, onw_ref[...])
        o_ref[rs, :] = h3


def _mixer(h, fox, ret, p, params, d_ff, tm, layer, final_norm):
    B, S, D = h.shape
    wo, fnw, wup, cw, cb, wd, pnw, wpg, wpp, onw = params
    tok = lambda w: pl.BlockSpec((None, tm, w), lambda b, s: (b, s, 0))
    res = lambda a: _layer_spec(a, layer)
    up_half = lambda j: pl.BlockSpec((None, D, d_ff), lambda b, s: (layer, 0, j), pipeline_mode=pl.Buffered(1))
    return pl.pallas_call(
        functools.partial(_mixer_kernel, final_norm=final_norm),
        grid=(B, S // tm),
        in_specs=[tok(D), tok(WIDTH), tok(WIDTH),
                  pl.BlockSpec((None, None, tm, p.shape[-1]), lambda b, s: (layer, b, s, 0)),
                  res(wo), res(fnw), up_half(0), up_half(1), res(cw), res(cb), res(wd), res(pnw), res(wpg),
                  res(wpp), _layer_spec(onw, 0)],
        out_specs=tok(D),
        out_shape=jax.ShapeDtypeStruct((B, S, D), F32),
        scratch_shapes=[pltpu.VMEM((2, tm + 8, FF_CHUNK), F32), pltpu.VMEM((8, d_ff), F32)],
        compiler_params=pltpu.CompilerParams(dimension_semantics=("parallel", "arbitrary"),
                                             vmem_limit_bytes=VMEM_LIMIT),
        name="channel_mixer",
    )(h, fox, ret, p, wo, fnw, wup, wup, cw, cb, wd, pnw, wpg, wpp, onw)


def _rotary_tables(seq):
    inv_freq = ROPE_BASE ** (-jnp.arange(0, HEAD_DIM, 2, dtype=F32) / HEAD_DIM)
    ang = jnp.arange(seq, dtype=F32)[:, None] * inv_freq[None, :]
    cos, sin = jnp.cos(ang), jnp.sin(ang)
    first = (jnp.arange(LANES) % HEAD_DIM) < HALF
    cosn = jnp.tile(cos, (1, LANES // HALF))
    sinn = jnp.tile(sin, (1, LANES // HALF))
    sina = jnp.where(first[None, :], -sinn, 0.0)
    sinb = jnp.where(first[None, :], 0.0, sinn)
    return cosn, sina, sinb, cos.T, sin.T


def _retention_tables(C):
    log_gamma = jnp.log1p(-jnp.exp2(-5.0 - jnp.arange(N_HEADS, dtype=F32)))
    idx = jnp.arange(C, dtype=F32)
    rel = idx[:, None] - idx[None, :]
    lg = log_gamma[:, None, None]
    dm = jnp.where(rel >= 0, jnp.exp(lg * jnp.maximum(rel, 0.0)), 0.0).reshape(N_PAIRS, 2, C, C)
    q_decay = jnp.exp(log_gamma[:, None] * (idx + 1.0))
    k_decay = jnp.exp(log_gamma[:, None] * (C - 1.0 - idx))
    chunk_decay = jnp.exp(log_gamma * C)
    qd = jnp.repeat(q_decay.reshape(N_PAIRS, 2, C), HEAD_DIM, axis=1).transpose(0, 2, 1)
    kd = jnp.repeat(k_decay.reshape(N_PAIRS, 2, C), HEAD_DIM, axis=1)
    blk = jnp.arange(LANES) // HEAD_DIM
    bd = (blk[:, None] == blk[None, :]).astype(F32)
    cd = jnp.repeat(chunk_decay.reshape(N_PAIRS, 2), HEAD_DIM, axis=1)[:, :, None] * bd[None]
    gn = (bd / HEAD_DIM).astype(BF16)
    return dm, qd, kd, cd, bd, gn


def kernel(x, p, attn_norm_w, w_in, forget_bias, w_out, ffn_norm_w, w_up, conv_w, conv_b, w_down, ple_norm_w,
           w_ple_gate, w_ple_proj, final_norm_w):
    B, S, D = x.shape
    depth = w_in.shape[0]
    d_ff = w_down.shape[1]
    tm = min(SEQ_TILE, S)
    assert S % tm == 0 and tm % RET_CHUNK == 0 and D == 2 * WIDTH
    scale = HEAD_DIM ** -0.5
    rot = _rotary_tables(S)
    rtab = _retention_tables(RET_CHUNK)

    o = [0, WIDTH, 2 * WIDTH, 3 * WIDTH, 3 * WIDTH + N_HEADS]
    o += [o[-1] + WIDTH, o[-1] + 2 * WIDTH, o[-1] + 3 * WIDTH, o[-1] + 4 * WIDTH]
    rows = lambda v: v.reshape(depth, 1, -1)
    col = lambda k: w_in[:, :, o[k]:o[k + 1]]
    wn = jnp.concatenate([col(1), col(4), col(6), col(7)], axis=2).astype(BF16)
    wt = jnp.concatenate([col(0) * (scale * LOG2E), col(2), col(5) * scale, col(3),
                          jnp.zeros((depth, D, BF16_ROWS - N_HEADS), F32)], axis=2).transpose(0, 2, 1).astype(BF16)
    mixer_params = (w_out.astype(BF16), rows(ffn_norm_w), w_up.astype(BF16), conv_w, rows(conv_b),
                    w_down.astype(BF16), rows(ple_norm_w), w_ple_gate.astype(BF16), w_ple_proj.astype(BF16),
                    final_norm_w.reshape(1, 1, -1))
    attn_nw, fbias = rows(attn_norm_w), forget_bias.reshape(depth, N_HEADS, 1)
    h = x
    for i in range(depth):
        fk, rq, rv, rg, fqt, fvt, rkt, cum = _inproj(h, attn_nw, wn, wt, fbias, rot, tm, i)
        fox = _fox(fk, fqt, fvt, cum)
        ret = _ret(rq, rkt, rv, rg, rtab)
        h = _mixer(h, fox, ret, p, mixer_params, d_ff, tm, i, final_norm=(i == depth - 1))
    return h
```

```python
import functools

import numpy as np
import jax
import jax.numpy as jnp
from jax import lax
from jax.experimental import pallas as pl
from jax.experimental.pallas import tpu as pltpu

F32 = jnp.float32
BF16 = jnp.bfloat16

HEAD_DIM = 64
HALF = HEAD_DIM // 2
N_HEADS = 8
WIDTH = N_HEADS * HEAD_DIM
LANES = 128
N_PAIRS = WIDTH // LANES
BF16_ROWS = 16
N_AUG = 3
CONV_WIDTH = 3
ROPE_BASE = 10000.0
NORM_EPS = 1e-6
GN_EPS = 1e-5
LOG2E = float(np.log2(np.e))
SEQ_TILE = 512
FOX_SUB = 256
FOX_QTILES = 4
RET_CHUNK = 256
RET_TILES = 4
MIX_ROWS = 256
FF_CHUNK = 1024
VMEM_LIMIT = 60 * 1024 * 1024


def _rms(x, w):
    return x * lax.rsqrt(jnp.mean(x * x, axis=-1, keepdims=True) + NORM_EPS) * w


def _split3(x):
    hi = x.astype(BF16).astype(F32)
    r = x - hi
    mid = r.astype(BF16).astype(F32)
    lo = (r - mid).astype(BF16).astype(F32)
    return hi, mid, lo


def _inproj_kernel(h_ref, nw_ref, wn_ref, wt_ref, fb_ref, cosn_ref, sina_ref, sinb_ref, cost_ref, sint_ref,
                   fk_ref, rq_ref, rv_ref, rg_ref, fqt_ref, fvt_ref, rkt_ref, cum_ref, carry_ref):
    tm = h_ref.shape[0]

    @pl.when(pl.program_id(1) == 0)
    def _():
        carry_ref[...] = jnp.zeros_like(carry_ref)

    logits = []
    for r in range(0, tm, MIX_ROWS):
        rs = slice(r, r + MIX_ROWS)
        u = _rms(h_ref[rs, :], nw_ref[...]).astype(BF16)
        pn = jnp.dot(u, wn_ref[...], preferred_element_type=F32)
        pt = lax.dot_general(wt_ref[...], u, (((1,), (1,)), ((), ())),
                             preferred_element_type=F32)

        fk_ref[rs, :] = pn[:, 0:WIDTH].astype(BF16)
        rv_ref[rs, :] = pn[:, 2 * WIDTH:3 * WIDTH].astype(BF16)
        rg = pn[:, 3 * WIDTH:4 * WIDTH]
        rg_ref[rs, :] = (rg * jax.nn.sigmoid(rg)).astype(BF16)

        cosn, sina, sinb = cosn_ref[rs, :], sina_ref[rs, :], sinb_ref[rs, :]
        for g in range(N_PAIRS):
            xg = pn[:, WIDTH + g * LANES:WIDTH + (g + 1) * LANES]
            o = xg * cosn + pltpu.roll(xg, LANES - HALF, 1) * sina + pltpu.roll(xg, HALF, 1) * sinb
            rq_ref[rs, g * LANES:(g + 1) * LANES] = o.astype(BF16)

        fqt_ref[:, rs] = pt[0:WIDTH].astype(BF16)
        fvt_ref[:, rs] = pt[WIDTH:2 * WIDTH].astype(BF16)
        cost, sint = cost_ref[:, rs], sint_ref[:, rs]
        for hd in range(N_HEADS):
            r0 = 2 * WIDTH + hd * HEAD_DIM
            t1, t2 = pt[r0:r0 + HALF], pt[r0 + HALF:r0 + HEAD_DIM]
            rkt_ref[hd * HEAD_DIM:hd * HEAD_DIM + HALF, rs] = (t1 * cost - t2 * sint).astype(BF16)
            rkt_ref[hd * HEAD_DIM + HALF:(hd + 1) * HEAD_DIM, rs] = (t1 * sint + t2 * cost).astype(BF16)
        logits.append(pt[3 * WIDTH:3 * WIDTH + N_HEADS])

    logit = jnp.concatenate(logits, axis=1) + fb_ref[...]
    logf = jnp.minimum(logit, 0.0) - jnp.log1p(jnp.exp(-jnp.abs(logit)))
    parts = jnp.concatenate(list(_split3(logf)) + [jnp.zeros_like(logf)], axis=0).astype(BF16)
    row = lax.broadcasted_iota(jnp.int32, (tm, tm), 0)
    col = lax.broadcasted_iota(jnp.int32, (tm, tm), 1)
    tri = jnp.where(row <= col, 1.0, 0.0).astype(BF16)
    cs = jnp.dot(parts, tri, preferred_element_type=F32)
    cum_ref[...] = cs[0:8] + cs[8:16] + cs[16:24] + carry_ref[...]
    carry_ref[...] = carry_ref[...] + jnp.sum(logf, axis=1, keepdims=True)


def _layer_spec(a, layer):
    return pl.BlockSpec((None,) + a.shape[1:], lambda b, s: (layer,) + (0,) * (a.ndim - 1),
                        pipeline_mode=pl.Buffered(1))


def _inproj(h, nw, wn, wt, fb, tabs, tm, layer):
    B, S, D = h.shape
    ns = S // tm
    cosn, sina, sinb, cost, sint = tabs
    tok = lambda w: pl.BlockSpec((None, tm, w), lambda b, s: (b, s, 0))
    full = lambda a: _layer_spec(a, layer)
    ttile = pl.BlockSpec((None, None, WIDTH, tm), lambda b, s: (b, s, 0, 0))
    nat = jax.ShapeDtypeStruct((B, S, WIDTH), BF16)
    ttl = jax.ShapeDtypeStruct((B, ns, WIDTH, tm), BF16)
    return pl.pallas_call(
        _inproj_kernel,
        grid=(B, ns),
        in_specs=[tok(D), full(nw), full(wn), full(wt), full(fb),
                  pl.BlockSpec((tm, LANES), lambda b, s: (s, 0)),
                  pl.BlockSpec((tm, LANES), lambda b, s: (s, 0)),
                  pl.BlockSpec((tm, LANES), lambda b, s: (s, 0)),
                  pl.BlockSpec((HALF, tm), lambda b, s: (0, s)),
                  pl.BlockSpec((HALF, tm), lambda b, s: (0, s))],
        out_specs=[tok(WIDTH), tok(WIDTH), tok(WIDTH), tok(WIDTH), ttile, ttile, ttile,
                   pl.BlockSpec((None, None, N_HEADS, tm), lambda b, s: (b, s, 0, 0))],
        out_shape=[nat, nat, nat, nat, ttl, ttl, ttl, jax.ShapeDtypeStruct((B, ns, N_HEADS, tm), F32)],
        scratch_shapes=[pltpu.VMEM((N_HEADS, 1), F32)],
        compiler_params=pltpu.CompilerParams(dimension_semantics=("parallel", "arbitrary"),
                                             vmem_limit_bytes=VMEM_LIMIT),
        name="inproj",
    )(h, nw, wn, wt, fb, cosn, sina, sinb, cost, sint)


def _fox_kernel(k_ref, qt_ref, vt_ref, cum_ref, o_ref, kaug_ref, vaug_ref, st_ref, m_ref, acc_ref):
    nk, _, tk = vt_ref.shape
    frow = lax.broadcasted_iota(jnp.int32, (LANES, tk), 0)
    klane = lax.broadcasted_iota(jnp.int32, (tk, LANES), 1)
    aug0 = (HEAD_DIM, 0)

    def build(j, carry):
        k0 = pl.multiple_of(j * tk, tk)
        kt = k_ref[pl.ds(k0, tk), :].astype(F32)
        vt = vt_ref[j].astype(F32)
        terms = _split3(cum_ref[j] * (-LOG2E))
        for hd in range(2):
            own = (klane < HEAD_DIM) if hd == 0 else (klane >= HEAD_DIM)
            own_t = (frow < HEAD_DIM) if hd == 0 else (frow >= HEAD_DIM)
            a = jnp.zeros((LANES, tk), F32)
            for t in range(N_AUG):
                a = jnp.where(frow == aug0[hd] + t, terms[t][hd:hd + 1], a)
            kaug_ref[hd, pl.ds(k0, tk), :] = (jnp.where(own, kt, 0.0) + a.T).astype(BF16)
            vaug_ref[hd, j] = jnp.where(own_t, vt, 1.0).astype(BF16)
        return carry

    lax.fori_loop(0, nk, build, 0)

    sub = FOX_SUB
    ks_per = tk // sub
    qs_per = FOX_QTILES * ks_per
    tq = FOX_QTILES * tk
    qrow = lax.broadcasted_iota(jnp.int32, (LANES, tk), 0)
    full_jobs = [(hd, qh) for qh in range(qs_per) for hd in range(2)]

    def diag_jobs(d):
        return [(hd, qh) for (hd, qh) in full_jobs if d * ks_per <= qh]

    def q_block(qi, carry):
        qa = []
        for t in range(FOX_QTILES):
            qt = qt_ref[qi * FOX_QTILES + t].astype(F32)
            qa.append([jnp.where(qrow < HEAD_DIM, qt, jnp.where(qrow < HEAD_DIM + N_AUG, 1.0, 0.0)).astype(BF16),
                       jnp.where(qrow >= HEAD_DIM, qt, jnp.where(qrow < N_AUG, 1.0, 0.0)).astype(BF16)])
        m_ref[...] = jnp.full(m_ref.shape, -jnp.inf, F32)
        acc_ref[...] = jnp.zeros(acc_ref.shape, F32)

        def live_keys(qh, diag):
            return tk if diag is None else min(tk, (qh - diag * ks_per + 1) * sub)

        def issue_qk(slot, kj, job, diag=None):
            hd, qh = job
            nkeys = live_keys(qh, diag)
            ks = kaug_ref[hd, pl.ds(pl.multiple_of(kj * tk, tk), nkeys), :]
            qsub = qa[qh // ks_per][hd][:, (qh % ks_per) * sub:(qh % ks_per + 1) * sub]
            st_ref[slot, full_jobs.index(job), 0:nkeys, :] = jnp.dot(ks, qsub,
                                                                     preferred_element_type=F32)

        def consume(slot, kj, job, diag):
            hd, qh = job
            nkeys = live_keys(qh, diag)
            st = st_ref[slot, full_jobs.index(job), 0:nkeys, :]
            if diag is not None and (diag + 1) * ks_per > qh:
                ahead = (lax.broadcasted_iota(jnp.int32, (nkeys, sub), 0)
                         - lax.broadcasted_iota(jnp.int32, (nkeys, sub), 1))
                st = jnp.where(ahead <= qh * sub - diag * tk, st, -jnp.inf)
            qs = slice(qh * sub, (qh + 1) * sub)
            m_prev = m_ref[hd, :, qs]
            m_new = jnp.maximum(m_prev, jnp.max(st, axis=0, keepdims=True))
            m_ref[hd, :, qs] = m_new
            alpha = jnp.exp2(m_prev - m_new)
            p = jnp.exp2(st - m_new).astype(BF16)
            acc_ref[hd, :, qs] = alpha * acc_ref[hd, :, qs] + jnp.dot(vaug_ref[hd, kj, :, 0:nkeys], p,
                                                                      preferred_element_type=F32)

        def step(slot, kj, jobs, diag, next_jobs, next_diag=None):
            for n in range(max(len(jobs), len(next_jobs))):
                if n < len(next_jobs):
                    issue_qk(1 - slot, kj + 1, next_jobs[n], next_diag)
                if n < len(jobs):
                    consume(slot, kj, jobs[n], diag)

        kbase = qi * FOX_QTILES
        for job in full_jobs:
            issue_qk(0, 0, job)

        def k_body(t, c):
            for r in range(FOX_QTILES):
                step(r % 2, t * FOX_QTILES + r, full_jobs, None, full_jobs)
            return c

        lax.fori_loop(0, qi, k_body, 0)
        for d in range(FOX_QTILES):
            jobs = diag_jobs(d)
            step(d % 2, kbase + d, jobs, d, diag_jobs(d + 1) if d + 1 < FOX_QTILES else [], d + 1)
        a0, a1 = acc_ref[0], acc_ref[1]
        orow = lax.broadcasted_iota(jnp.int32, a0.shape, 0)
        ot = jnp.where(orow < HEAD_DIM, a0 / a0[HEAD_DIM:HEAD_DIM + 1], a1 / a1[0:1])
        o_ref[pl.ds(pl.multiple_of(qi * tq, tq), tq), :] = ot.T.astype(o_ref.dtype)
        return carry

    lax.fori_loop(0, nk // FOX_QTILES, q_block, 0)


def _fox(fk, fqt, fvt, cum):
    B, S, _ = fk.shape
    _, nk, _, tk = fqt.shape
    cum5 = cum.reshape(B, nk, N_PAIRS, 2, tk)
    assert FOX_QTILES % 2 == 0 and nk % FOX_QTILES == 0 and tk % FOX_SUB == 0
    tq = FOX_QTILES * tk
    n_jobs = tq // FOX_SUB
    nat = pl.BlockSpec((None, S, LANES), lambda b, p: (b, 0, p))
    ttile = pl.BlockSpec((None, nk, LANES, tk), lambda b, p: (b, 0, p, 0))
    return pl.pallas_call(
        _fox_kernel,
        grid=(B, N_PAIRS),
        in_specs=[nat, ttile, ttile,
                  pl.BlockSpec((None, nk, None, 2, tk), lambda b, p: (b, 0, p, 0, 0))],
        out_specs=nat,
        out_shape=jax.ShapeDtypeStruct((B, S, WIDTH), BF16),
        scratch_shapes=[pltpu.VMEM((2, S, LANES), BF16), pltpu.VMEM((2, nk, LANES, tk), BF16),
                        pltpu.VMEM((2, 2 * n_jobs, tk, FOX_SUB), F32),
                        pltpu.VMEM((2, 1, tq), F32), pltpu.VMEM((2, LANES, tq), F32)],
        compiler_params=pltpu.CompilerParams(dimension_semantics=("parallel", "parallel"),
                                             vmem_limit_bytes=VMEM_LIMIT),
        name="fox_attention",
    )(fk, fqt, fvt, cum5)


def _ret_kernel(q_ref, kt_ref, v_ref, g_ref, dm_ref, qd_ref, kd_ref, cd_ref, bd_ref, gn_ref, o_ref, st_ref):
    nk, _, tk = kt_ref.shape
    C = dm_ref.shape[-1]
    lane = lax.broadcasted_iota(jnp.int32, (C, LANES), 1)
    st_ref[...] = jnp.zeros_like(st_ref)
    gn = gn_ref[...]

    def group_mean(y):
        y_hi = y.astype(BF16)
        y_lo = (y - y_hi.astype(F32)).astype(BF16)
        return jnp.dot(y_hi, gn, preferred_element_type=F32) + jnp.dot(y_lo, gn, preferred_element_type=F32)

    def trip(j, carry):
        chunks = []
        for t in range(RET_TILES):
            kt_tile = kt_ref[j * RET_TILES + t]
            for sc in range(tk // C):
                r0 = pl.multiple_of((j * RET_TILES + t) * tk + sc * C, C)
                chunks.append((r0, q_ref[pl.ds(r0, C), :], v_ref[pl.ds(r0, C), :],
                               kt_tile[:, sc * C:(sc + 1) * C]))
        scores, kvs = [], []
        for r0, q2, v2, kt in chunks:
            q2f = q2.astype(F32)
            sc_pair = []
            for hd in range(2):
                keep = (lane < HEAD_DIM) if hd == 0 else (lane >= HEAD_DIM)
                qm = jnp.where(keep, q2f, 0.0).astype(BF16)
                sc_pair.append((jnp.dot(qm, kt, preferred_element_type=F32) * dm_ref[hd]).astype(BF16))
            scores.append(sc_pair)
            kdec = (kt.astype(F32) * kd_ref[...]).astype(BF16)
            kvs.append(jnp.dot(kdec, v2, preferred_element_type=F32) * bd_ref[...])
        st = st_ref[...]
        crosses = []
        for (r0, q2, v2, kt), kv in zip(chunks, kvs):
            crosses.append(jnp.dot(q2, st.astype(BF16), preferred_element_type=F32) * qd_ref[...])
            st = st * cd_ref[...] + kv
        st_ref[...] = st
        ys = []
        for (r0, q2, v2, kt), sc_pair, cross in zip(chunks, scores, crosses):
            inner = [jnp.dot(s, v2, preferred_element_type=F32) for s in sc_pair]
            ys.append(jnp.where(lane < HEAD_DIM, inner[0], inner[1]) + cross)
        ds = [y - group_mean(y) for y in ys]
        vs = [group_mean(d * d) for d in ds]
        for (r0, q2, v2, kt), d, var in zip(chunks, ds, vs):
            out = d * lax.rsqrt(var + GN_EPS) * g_ref[pl.ds(r0, C), :].astype(F32)
            o_ref[pl.ds(r0, C), :] = out.astype(o_ref.dtype)
        return carry

    lax.fori_loop(0, nk // RET_TILES, trip, 0)


def _ret(rq, rkt, rv, rg, tabs):
    B, S, _ = rq.shape
    _, nk, _, tk = rkt.shape
    dm, qd, kd, cd, bd, gn = tabs
    C = dm.shape[-1]
    assert nk % RET_TILES == 0 and tk % C == 0
    nat = pl.BlockSpec((None, S, LANES), lambda b, p: (b, 0, p))
    return pl.pallas_call(
        _ret_kernel,
        grid=(B, N_PAIRS),
        in_specs=[nat,
                  pl.BlockSpec((None, nk, LANES, tk), lambda b, p: (b, 0, p, 0)),
                  nat, nat,
                  pl.BlockSpec((None, 2, C, C), lambda b, p: (p, 0, 0, 0)),
                  pl.BlockSpec((None, C, LANES), lambda b, p: (p, 0, 0)),
                  pl.BlockSpec((None, LANES, C), lambda b, p: (p, 0, 0)),
                  pl.BlockSpec((None, LANES, LANES), lambda b, p: (p, 0, 0)),
                  pl.BlockSpec((LANES, LANES), lambda b, p: (0, 0)),
                  pl.BlockSpec((LANES, LANES), lambda b, p: (0, 0))],
        out_specs=nat,
        out_shape=jax.ShapeDtypeStruct((B, S, WIDTH), BF16),
        scratch_shapes=[pltpu.VMEM((LANES, LANES), F32)],
        compiler_params=pltpu.CompilerParams(dimension_semantics=("parallel", "parallel"),
                                             vmem_limit_bytes=VMEM_LIMIT),
        name="retention",
    )(rq, rkt, rv, rg, dm, qd, kd, cd, bd, gn)


def _mixer_kernel(h_ref, fox_ref, ret_ref, p_ref, wo_ref, fnw_ref, wa_ref, wg_ref, cw_ref, cb_ref, wd_ref,
                  pnw_ref, wpg_ref, wpp_ref, onw_ref, o_ref, abuf_ref, tail_ref, *, final_norm):
    tm = h_ref.shape[0]
    d_ff = wa_ref.shape[1]
    halo = tail_ref.shape[0]

    @pl.when(pl.program_id(1) == 0)
    def _():
        tail_ref[...] = jnp.zeros_like(tail_ref)

    strands = [slice(r, r + MIX_ROWS) for r in range(0, tm, MIX_ROWS)]
    h1s, us = [], []
    for rs in strands:
        h1 = (h_ref[rs, :] + jnp.dot(fox_ref[rs, :], wo_ref[0:WIDTH], preferred_element_type=F32)
              + jnp.dot(ret_ref[rs, :], wo_ref[WIDTH:2 * WIDTH], preferred_element_type=F32))
        h1s.append(h1)
        us.append(_rms(h1, fnw_ref[...]).astype(BF16))
    downs = [None] * len(strands)
    for c0 in range(0, d_ff, FF_CHUNK):
        c1 = min(c0 + FF_CHUNK, d_ff)
        w = c1 - c0
        abuf = abuf_ref.at[(c0 // FF_CHUNK) % 2]
        abuf[0:halo, 0:w] = tail_ref[:, c0:c1]
        gs = []
        for i, rs in enumerate(strands):
            a = jnp.dot(us[i], wa_ref[:, c0:c1], preferred_element_type=F32)
            gs.append(jnp.dot(us[i], wg_ref[:, c0:c1], preferred_element_type=F32))
            abuf[halo + rs.start:halo + rs.stop, 0:w] = a
            if rs.stop == tm:
                tail_ref[:, c0:c1] = a[MIX_ROWS - halo:MIX_ROWS]
        for i, rs in enumerate(strands):
            y = cb_ref[:, c0:c1]
            for t in range(CONV_WIDTH):
                off = halo - (CONV_WIDTH - 1) + t + rs.start
                y = y + abuf[off:off + MIX_ROWS, 0:w] * cw_ref[t:t + 1, c0:c1]
            act = (0.5 * y * (1.0 + lax.erf(y * np.float32(1.0 / np.sqrt(2.0)))) * gs[i]).astype(BF16)
            part = jnp.dot(act, wd_ref[c0:c1, :], preferred_element_type=F32)
            downs[i] = part if downs[i] is None else downs[i] + part
    for i, rs in enumerate(strands):
        h2 = h1s[i] + downs[i]
        n = _rms(h2, pnw_ref[...]).astype(BF16)
        gate = jax.nn.sigmoid(jnp.dot(n, wpg_ref[...], preferred_element_type=F32))
        pe = jnp.dot(p_ref[rs, :].astype(BF16), wpp_ref[...], preferred_element_type=F32)
        h3 = h2 + gate * pe
        if final_norm:
            h3 = _rms(h3, onw_ref[...])
        o_ref[rs, :] = h3


def _mixer(h, fox, ret, p, params, d_ff, tm, layer, final_norm):
    B, S, D = h.shape
    wo, fnw, wup, cw, cb, wd, pnw, wpg, wpp, onw = params
    tok = lambda w: pl.BlockSpec((None, tm, w), lambda b, s: (b, s, 0))
    res = lambda a: _layer_spec(a, layer)
    up_half = lambda j: pl.BlockSpec((None, D, d_ff), lambda b, s: (layer, 0, j), pipeline_mode=pl.Buffered(1))
    return pl.pallas_call(
        functools.partial(_mixer_kernel, final_norm=final_norm),
        grid=(B, S // tm),
        in_specs=[tok(D), tok(WIDTH), tok(WIDTH),
                  pl.BlockSpec((None, None, tm, p.shape[-1]), lambda b, s: (layer, b, s, 0)),
                  res(wo), res(fnw), up_half(0), up_half(1), res(cw), res(cb), res(wd), res(pnw), res(wpg),
                  res(wpp), _layer_spec(onw, 0)],
        out_specs=tok(D),
        out_shape=jax.ShapeDtypeStruct((B, S, D), F32),
        scratch_shapes=[pltpu.VMEM((2, tm + 8, FF_CHUNK), F32), pltpu.VMEM((8, d_ff), F32)],
        compiler_params=pltpu.CompilerParams(dimension_semantics=("parallel", "arbitrary"),
                                             vmem_limit_bytes=VMEM_LIMIT),
        name="channel_mixer",
    )(h, fox, ret, p, wo, fnw, wup, wup, cw, cb, wd, pnw, wpg, wpp, onw)


def _rotary_tables(seq):
    inv_freq = ROPE_BASE ** (-jnp.arange(0, HEAD_DIM, 2, dtype=F32) / HEAD_DIM)
    ang = jnp.arange(seq, dtype=F32)[:, None] * inv_freq[None, :]
    cos, sin = jnp.cos(ang), jnp.sin(ang)
    first = (jnp.arange(LANES) % HEAD_DIM) < HALF
    cosn = jnp.tile(cos, (1, LANES // HALF))
    sinn = jnp.tile(sin, (1, LANES // HALF))
    sina = jnp.where(first[None, :], -sinn, 0.0)
    sinb = jnp.where(first[None, :], 0.0, sinn)
    return cosn, sina, sinb, cos.T, sin.T


def _retention_tables(C):
    log_gamma = jnp.log1p(-jnp.exp2(-5.0 - jnp.arange(N_HEADS, dtype=F32)))
    idx = jnp.arange(C, dtype=F32)
    rel = idx[:, None] - idx[None, :]
    lg = log_gamma[:, None, None]
    dm = jnp.where(rel >= 0, jnp.exp(lg * jnp.maximum(rel, 0.0)), 0.0).reshape(N_PAIRS, 2, C, C)
    q_decay = jnp.exp(log_gamma[:, None] * (idx + 1.0))
    k_decay = jnp.exp(log_gamma[:, None] * (C - 1.0 - idx))
    chunk_decay = jnp.exp(log_gamma * C)
    qd = jnp.repeat(q_decay.reshape(N_PAIRS, 2, C), HEAD_DIM, axis=1).transpose(0, 2, 1)
    kd = jnp.repeat(k_decay.reshape(N_PAIRS, 2, C), HEAD_DIM, axis=1)
    blk = jnp.arange(LANES) // HEAD_DIM
    bd = (blk[:, None] == blk[None, :]).astype(F32)
    cd = jnp.repeat(chunk_decay.reshape(N_PAIRS, 2), HEAD_DIM, axis=1)[:, :, None] * bd[None]
    gn = (bd / HEAD_DIM).astype(BF16)
    return dm, qd, kd, cd, bd, gn


def kernel(x, p, attn_norm_w, w_in, forget_bias, w_out, ffn_norm_w, w_up, conv_w, conv_b, w_down, ple_norm_w,
           w_ple_gate, w_ple_proj, final_norm_w):
    B, S, D = x.shape
    depth = w_in.shape[0]
    d_ff = w_down.shape[1]
    tm = min(SEQ_TILE, S)
    assert S % tm == 0 and tm % RET_CHUNK == 0 and D == 2 * WIDTH
    scale = HEAD_DIM ** -0.5
    rot = _rotary_tables(S)
    rtab = _retention_tables(RET_CHUNK)

    o = [0, WIDTH, 2 * WIDTH, 3 * WIDTH, 3 * WIDTH + N_HEADS]
    o += [o[-1] + WIDTH, o[-1] + 2 * WIDTH, o[-1] + 3 * WIDTH, o[-1] + 4 * WIDTH]
    rows = lambda v: v.reshape(depth, 1, -1)
    col = lambda k: w_in[:, :, o[k]:o[k + 1]]
    wn = jnp.concatenate([col(1), col(4), col(6), col(7)], axis=2).astype(BF16)
    wt = jnp.concatenate([col(0) * (scale * LOG2E), col(2), col(5) * scale, col(3),
                          jnp.zeros((depth, D, BF16_ROWS - N_HEADS), F32)], axis=2).transpose(0, 2, 1).astype(BF16)
    mixer_params = (w_out.astype(BF16), rows(ffn_norm_w), w_up.astype(BF16), conv_w, rows(conv_b),
                    w_down.astype(BF16), rows(ple_norm_w), w_ple_gate.astype(BF16), w_ple_proj.astype(BF16),
                    final_norm_w.reshape(1, 1, -1))
    attn_nw, fbias = rows(attn_norm_w), forget_bias.reshape(depth, N_HEADS, 1)
    h = x
    for i in range(depth):
        fk, rq, rv, rg, fqt, fvt, rkt, cum = _inproj(h, attn_nw, wn, wt, fbias, rot, tm, i)
        fox = _fox(fk, fqt, fvt, cum)
        ret = _ret(rq, rkt, rv, rg, rtab)
        h = _mixer(h, fox, ret, p, mixer_params, d_ff, tm, i, final_norm=(i == depth - 1))
    return h
```
